```python
import jax, jax.numpy as jnp
from jax import lax
import numpy as np

D_MODEL = 1024
BATCH = 8
SEQ = 4096
DEPTH = 1
DEC_BATCH = 2
DEC_SEQ = 8192
PAST_LEN = 128

DN_HEADS = 8
DN_DK = 128
DN_DV = 128
DN_CONV = 5
DN_CHUNK = 64
AT_HEADS = 8
AT_KV_HEADS = 2
AT_HEAD_DIM = 128
AT_WINDOW = 128
AT_BLOCK = 128
ROPE_THETA = 500000.0
ROPE_DIM = AT_HEAD_DIM // 4
D_FF = 2816
EPS = 1e-6
N_MOD = 9

DN_QK_W = DN_HEADS * DN_DK
DN_V_W = DN_HEADS * DN_DV
DN_QKV_W = 2 * DN_QK_W + DN_V_W
AT_Q_W = AT_HEADS * AT_HEAD_DIM
AT_KV_W = AT_KV_HEADS * AT_HEAD_DIM
MIX_SPLITS = (DN_QKV_W, DN_V_W, 2 * DN_HEADS, 2 * DN_HEADS, AT_Q_W, AT_KV_W, AT_KV_W, D_MODEL, D_MODEL)
PROJ_W = DN_QKV_W + DN_V_W + 4 * DN_HEADS + AT_Q_W + 2 * AT_KV_W + 2 * D_MODEL

kernel_name = "hybrid_deltanet_swa_macaron_encoder"


def rmsnorm(x, w):
    xf = x.astype(jnp.float32)
    y = xf * lax.rsqrt(jnp.mean(xf * xf, axis=-1, keepdims=True) + EPS)
    return (y * w.astype(jnp.float32)).astype(x.dtype)


def l2norm(x):
    xf = x.astype(jnp.float32)
    return xf * lax.rsqrt(jnp.sum(xf * xf, axis=-1, keepdims=True) + EPS)


def swiglu(x, w_in, w_out):
    g, u = jnp.split(x @ w_in, 2, axis=-1)
    return (jax.nn.silu(g) * u) @ w_out


def partial_rope(x, pos):
    half = ROPE_DIM // 2
    inv = jnp.power(jnp.float32(ROPE_THETA), -jnp.arange(half, dtype=jnp.float32) / half)
    ang = pos.astype(jnp.float32)[:, None] * inv[None, :]
    cos = jnp.cos(ang)[None, :, None, :]
    sin = jnp.sin(ang)[None, :, None, :]
    xr = x[..., :ROPE_DIM].astype(jnp.float32)
    x1, x2 = xr[..., :half], xr[..., half:]
    rot = jnp.concatenate([x1 * cos - x2 * sin, x2 * cos + x1 * sin], axis=-1).astype(x.dtype)
    return jnp.concatenate([rot, x[..., ROPE_DIM:]], axis=-1)


def centred_depthwise_conv(x, w):
    C = x.shape[-1]
    return lax.conv_general_dilated(
        x, w[:, None, :].astype(x.dtype), window_strides=(1,),
        padding=[(DN_CONV // 2, DN_CONV // 2)],
        dimension_numbers=('NWC', 'WIO', 'NWC'), feature_group_count=C)


def gated_delta_chunked(q, k, v, beta, g):
    B, T, H, dk = q.shape
    dv = v.shape[-1]
    C = DN_CHUNK
    N = T // C

    def chunks(a):
        a = a.reshape((B, N, C, H) + a.shape[3:])
        perm = (1, 0, 3, 2) + tuple(range(4, a.ndim))
        return a.transpose(perm)

    q = chunks(q * (dk ** -0.5))
    k = chunks(k)
    v = chunks(v)
    beta = chunks(beta)
    gc = jnp.cumsum(chunks(g), axis=-1)
    tri = jnp.tril(jnp.ones((C, C), dtype=bool))
    strict = jnp.tril(jnp.ones((C, C), dtype=bool), -1)
    decay = jnp.exp(jnp.where(tri, gc[..., :, None] - gc[..., None, :], -jnp.inf))
    kb = k * beta[..., None]
    vb = v * beta[..., None]
    L = jnp.where(strict, jnp.einsum('nbhid,nbhjd->nbhij', kb, k) * decay, 0.0)
    tmat = L + jnp.eye(C, dtype=jnp.float32)
    rhs = jnp.concatenate([vb, kb * jnp.exp(gc)[..., None]], axis=-1)
    sol = lax.linalg.triangular_solve(tmat, rhs, left_side=True, lower=True, unit_diagonal=True)
    u0, w = sol[..., :dv], sol[..., dv:]
    qk = jnp.einsum('nbhid,nbhjd->nbhij', q, k) * decay
    q_dec = q * jnp.exp(gc)[..., None]
    k_dec = k * jnp.exp(gc[..., -1:] - gc)[..., None]
    g_last = jnp.exp(gc[..., -1])

    def step(S, xs):
        qk_i, q_i, w_i, u_i, k_i, gl = xs
        u = u_i - jnp.einsum('bhcd,bhde->bhce', w_i, S)
        o = jnp.einsum('bhcd,bhde->bhce', q_i, S) + jnp.einsum('bhij,bhje->bhie', qk_i, u)
        S = S * gl[..., None, None] + jnp.einsum('bhcd,bhce->bhde', k_i, u)
        return S, o

    S0 = jnp.zeros((B, H, dk, dv), jnp.float32)
    _, o = lax.scan(step, S0, (qk, q_dec, w, u0, k_dec, g_last))
    return o.transpose(1, 0, 3, 2, 4).reshape(B, T, H, dv)


def deltanet_branch(qkv, z, b_raw, a_raw, conv_w, a_log, dt_bias, norm_w):
    B, T, _ = qkv.shape
    qkv = jax.nn.silu(centred_depthwise_conv(qkv, conv_w))
    q, k, v = jnp.split(qkv, [DN_QK_W, 2 * DN_QK_W], axis=-1)
    q = l2norm(q.reshape(B, T, DN_HEADS, DN_DK))
    k = l2norm(k.reshape(B, T, DN_HEADS, DN_DK))
    v = v.reshape(B, T, DN_HEADS, DN_DV).astype(jnp.float32)
    beta = jax.nn.sigmoid(b_raw.astype(jnp.float32)).reshape(B, T, 2, DN_HEADS)
    g = -jnp.exp(a_log.astype(jnp.float32)) * jax.nn.softplus(
        a_raw.astype(jnp.float32).reshape(B, T, 2, DN_HEADS) + dt_bias.astype(jnp.float32))
    o_f = gated_delta_chunked(q, k, v, beta[:, :, 0], g[:, :, 0])
    rev = lambda a: jnp.flip(a, axis=1)
    o_b = rev(gated_delta_chunked(rev(q), rev(k), rev(v), rev(beta[:, :, 1]), rev(g[:, :, 1])))
    o = rmsnorm(o_f + o_b, norm_w) * jax.nn.silu(z.reshape(B, T, DN_HEADS, DN_DV).astype(jnp.float32))
    return o.reshape(B, T, DN_V_W).astype(qkv.dtype)


def window_attention(q, k, v, sink):
    B, T, H, hd = q.shape
    Hkv = k.shape[2]
    G = H // Hkv
    Wb = AT_BLOCK
    W = AT_WINDOW
    nb = T // Wb

    def band(a):
        ap = jnp.pad(a, ((0, 0), (W, W), (0, 0), (0, 0))).reshape(B, nb + 2, Wb, Hkv, hd)
        return jnp.concatenate([ap[:, :-2], ap[:, 1:-1], ap[:, 2:]], axis=2)

    kw = band(k)
    vw = band(v)
    qb = q.reshape(B, nb, Wb, Hkv, G, hd)
    s = jnp.einsum('bnqkgd,bnskd->bnkgqs', qb, kw).astype(jnp.float32) * (hd ** -0.5)
    qi = jnp.arange(Wb)[:, None]
    sj = jnp.arange(3 * Wb)[None, :]
    inband = jnp.abs(sj - W - qi) <= W
    kpos = jnp.arange(nb)[:, None, None] * Wb + sj[None] - W
    valid = inband[None] & (kpos >= 0) & (kpos < T)
    s = jnp.where(valid[None, :, None, None], s, -jnp.inf)
    sink_l = sink.astype(jnp.float32).reshape(Hkv, G)[None, None, :, :, None, None]
    m = jnp.maximum(jnp.max(s, axis=-1, keepdims=True), sink_l)
    p = jnp.exp(s - m)
    p = p / (jnp.sum(p, axis=-1, keepdims=True) + jnp.exp(sink_l - m))
    o = jnp.einsum('bnkgqs,bnskd->bnqkgd', p.astype(v.dtype), vw)
    return o.reshape(B, T, H * hd)


def token_mixer(h, w_in, conv_w, a_log, dt_bias, dn_norm, attn_sink, w_proj_a, w_proj_b, w_out):
    B, T, _ = h.shape
    offs = [int(o) for o in np.cumsum(MIX_SPLITS)[:-1]]
    dn_qkv, dn_z, dn_b, dn_a, at_q, at_k, at_v, gate_a, gate_b = jnp.split(h @ w_in, offs, axis=-1)
    y_a = deltanet_branch(dn_qkv, dn_z, dn_b, dn_a, conv_w, a_log, dt_bias, dn_norm) @ w_proj_a
    pos = jnp.arange(T)
    q = partial_rope(at_q.reshape(B, T, AT_HEADS, AT_HEAD_DIM), pos)
    k = partial_rope(at_k.reshape(B, T, AT_KV_HEADS, AT_HEAD_DIM), pos)
    v = at_v.reshape(B, T, AT_KV_HEADS, AT_HEAD_DIM)
    y_b = window_attention(q, k, v, attn_sink) @ w_proj_b
    merged = jax.nn.sigmoid(gate_a) * y_a + jax.nn.sigmoid(gate_b) * y_b
    return merged @ w_out


def ada_rms(h, w, shift, scale):
    return rmsnorm(h, w) * (1 + scale) + shift


def encoder_layer(x, c, w_ada, b_ada, ffn1_norm, ffn1_w_in, ffn1_w_out, mix_norm, w_in, conv_w,
                  a_log, dt_bias, dn_norm, attn_sink, w_proj_a, w_proj_b, w_out,
                  ffn2_norm, ffn2_w_in, ffn2_w_out):
    mod = (jax.nn.silu(c) @ w_ada + b_ada)[:, None, :]
    sh1, sc1, gt1, sh2, sc2, gt2, sh3, sc3, gt3 = jnp.split(mod, N_MOD, axis=-1)
    x = x + 0.5 * gt1 * swiglu(ada_rms(x, ffn1_norm, sh1, sc1), ffn1_w_in, ffn1_w_out)
    x = x + gt2 * token_mixer(ada_rms(x, mix_norm, sh2, sc2), w_in, conv_w, a_log, dt_bias,
                              dn_norm, attn_sink, w_proj_a, w_proj_b, w_out)
    x = x + 0.5 * gt3 * swiglu(ada_rms(x, ffn2_norm, sh3, sc3), ffn2_w_in, ffn2_w_out)
    return x


def setup_inputs(seed: int = 0) -> dict:
    key = jax.random.key(seed)
    ks = jax.random.split(key, 24)
    f32 = jnp.float32
    D = D_MODEL
    nrm = lambda k, shape, s: jax.random.normal(k, shape, f32) * s
    gain = lambda k, shape: 1.0 + 0.02 * jax.random.normal(k, shape, f32)
    dt = jnp.exp(jax.random.uniform(ks[12], (DEPTH, 2, DN_HEADS), f32, np.log(1e-3), np.log(1e-1)))
    return {
        "x_prompt": nrm(ks[0], (BATCH, SEQ, D), 1.0),
        "x_sample": nrm(ks[1], (DEC_BATCH, DEC_SEQ, D), 1.0),
        "c_prompt": nrm(ks[2], (BATCH, D), 1.0),
        "c_sample": nrm(ks[3], (DEC_BATCH, D), 1.0),
        "w_ada": nrm(ks[4], (DEPTH, D, N_MOD * D), 0.5 * D ** -0.5),
        "b_ada": nrm(ks[5], (DEPTH, N_MOD * D), 0.02),
        "ffn1_norm": gain(ks[6], (DEPTH, D)),
        "ffn1_w_in": nrm(ks[7], (DEPTH, D, 2 * D_FF), D ** -0.5),
        "ffn1_w_out": nrm(ks[8], (DEPTH, D_FF, D), D_FF ** -0.5),
        "mix_norm": gain(ks[9], (DEPTH, D)),
        "w_in": nrm(ks[10], (DEPTH, D, PROJ_W), D ** -0.5),
        "conv_w": nrm(ks[11], (DEPTH, DN_CONV, DN_QKV_W), DN_CONV ** -0.5),
        "a_log": jnp.log(jax.random.uniform(ks[13], (DEPTH, 2, DN_HEADS), f32, 1.0, 16.0)),
        "dt_bias": dt + jnp.log(-jnp.expm1(-dt)),
        "dn_norm": gain(ks[14], (DEPTH, DN_DV)),
        "attn_sink": nrm(ks[15], (DEPTH, AT_HEADS), 1.0),
        "w_proj_a": nrm(ks[16], (DEPTH, DN_V_W, D), DN_V_W ** -0.5),
        "w_proj_b": nrm(ks[17], (DEPTH, AT_Q_W, D), AT_Q_W ** -0.5),
        "w_out": nrm(ks[18], (DEPTH, D, D), D ** -0.5),
        "ffn2_norm": gain(ks[19], (DEPTH, D)),
        "ffn2_w_in": nrm(ks[20], (DEPTH, D, 2 * D_FF), D ** -0.5),
        "ffn2_w_out": nrm(ks[21], (DEPTH, D_FF, D), D_FF ** -0.5),
        "final_norm": gain(ks[22], (D,)),
    }


def reference(x_prompt, x_sample, c_prompt, c_sample, w_ada, b_ada, ffn1_norm, ffn1_w_in, ffn1_w_out,
              mix_norm, w_in, conv_w, a_log, dt_bias, dn_norm, attn_sink, w_proj_a, w_proj_b, w_out,
              ffn2_norm, ffn2_w_in, ffn2_w_out, final_norm):
    def trunk(x, c):
        for l in range(DEPTH):
            x = encoder_layer(x, c, w_ada[l], b_ada[l], ffn1_norm[l], ffn1_w_in[l], ffn1_w_out[l],
                              mix_norm[l], w_in[l], conv_w[l], a_log[l], dt_bias[l], dn_norm[l],
                              attn_sink[l], w_proj_a[l], w_proj_b[l], w_out[l],
                              ffn2_norm[l], ffn2_w_in[l], ffn2_w_out[l])
        return rmsnorm(x, final_norm)

    y_prompt = trunk(x_prompt, c_prompt)
    y_sample = trunk(x_sample, c_sample)
    return (y_prompt, y_sample)
```

```python
import functools
import math

import jax
import jax.numpy as jnp
from jax import lax
from jax.experimental import pallas as pl
from jax.experimental.pallas import tpu as pltpu

D_MODEL = 1024
DN_HEADS = 8
DN_DK = 128
DN_DV = 128
DN_CONV = 5
DN_CHUNK = 64
AT_HEADS = 8
AT_KV_HEADS = 2
AT_HEAD_DIM = 128
AT_WINDOW = 128
ROPE_THETA = 500000.0
ROPE_DIM = AT_HEAD_DIM // 4
D_FF = 2816
EPS = 1e-6
N_MOD = 9

DN_QK_W = DN_HEADS * DN_DK
DN_V_W = DN_HEADS * DN_DV
DN_QKV_W = 2 * DN_QK_W + DN_V_W
AT_Q_W = AT_HEADS * AT_HEAD_DIM
AT_KV_W = AT_KV_HEADS * AT_HEAD_DIM
AT_GROUP = AT_HEADS // AT_KV_HEADS

LANES = 128
BF16_SUBLANES = 16
VMEM_LIMIT = 56 * 1024 * 1024

F32 = jnp.float32
BF16 = jnp.bfloat16


def _dot(a, b):
    return jnp.dot(a, b, preferred_element_type=F32)


def _dot_nt(a, b):
    return lax.dot_general(a, b, (((1,), (1,)), ((), ())), preferred_element_type=F32)


def _dot_tn(a, b):
    return lax.dot_general(a, b, (((0,), (0,)), ((), ())), preferred_element_type=F32)


def _rms(x, w):
    return x * lax.rsqrt(jnp.mean(x * x, axis=-1, keepdims=True) + EPS) * w


def _ada_rms(x, w, shift, scale):
    return _rms(x, w) * (1.0 + scale) + shift


def _sigmoid(x):
    return 1.0 / (1.0 + jnp.exp(-x))


def _silu(x):
    return x * _sigmoid(x)


def _resident(shape):
    nd = len(shape)
    return pl.BlockSpec(shape, lambda *_: (0,) * nd, pipeline_mode=pl.Buffered(1))


def _params(sem):
    return pltpu.CompilerParams(dimension_semantics=sem, vmem_limit_bytes=VMEM_LIMIT)


def _mod_kernel(c_ref, w_ref, b_ref, o_ref):
    c = c_ref[...]
    o_ref[...] = _dot(_silu(c).astype(BF16), w_ref[...]) + b_ref[...]


def _mod_call(c, w_ada, b_ada):
    bsz = c.shape[0]
    n = w_ada.shape[1]
    bn = n // N_MOD
    return pl.pallas_call(
        _mod_kernel,
        grid=(N_MOD,),
        in_specs=[
            pl.BlockSpec((bsz, D_MODEL), lambda j: (0, 0)),
            pl.BlockSpec((D_MODEL, bn), lambda j: (0, j)),
            pl.BlockSpec((1, bn), lambda j: (0, j)),
        ],
        out_specs=pl.BlockSpec((bsz, bn), lambda j: (0, j)),
        out_shape=jax.ShapeDtypeStruct((bsz, n), F32),
        name="mod",
        compiler_params=_params(("arbitrary",)),
    )(c, w_ada, b_ada)


def _ffn_kernel(x_ref, mod_ref, nw_ref, wg_ref, wu_ref, wo_ref, fw_ref, o_ref, *, sub, final):
    x = x_ref[0]
    shift = mod_ref[0, 3 * sub:3 * sub + 1, :]
    scale = mod_ref[0, 3 * sub + 1:3 * sub + 2, :]
    gate = mod_ref[0, 3 * sub + 2:3 * sub + 3, :]
    h = _ada_rms(x, nw_ref[...], shift, scale).astype(BF16)
    g = _dot(h, wg_ref[...])
    u = _dot(h, wu_ref[...])
    a = (_silu(g) * u).astype(BF16)
    y = x + 0.5 * gate * _dot(a, wo_ref[...])
    if final:
        y = _rms(y, fw_ref[...])
    o_ref[0] = y


def _ffn_call(x, mod, norm_w, wg, wu, wo, final_w, *, sub, final, tm):
    bsz, t, d = x.shape
    return pl.pallas_call(
        functools.partial(_ffn_kernel, sub=sub, final=final),
        grid=(bsz, t // tm),
        in_specs=[
            pl.BlockSpec((1, tm, d), lambda b, i: (b, i, 0)),
            pl.BlockSpec((1, N_MOD, d), lambda b, i: (b, 0, 0)),
            _resident((1, d)),
            _resident(wg.shape),
            _resident(wu.shape),
            _resident(wo.shape),
            _resident((1, d)),
        ],
        out_specs=pl.BlockSpec((1, tm, d), lambda b, i: (b, i, 0)),
        out_shape=jax.ShapeDtypeStruct(x.shape, F32),
        name="ffn",
        compiler_params=_params(("parallel", "parallel")),
    )(x, mod, norm_w, wg, wu, wo, final_w)


def _rope(x, cos, sin_lo, sin_hi):
    half = ROPE_DIM // 2
    up = pltpu.roll(x, LANES - half, 1)
    down = pltpu.roll(x, half, 1)
    return x * cos + up * sin_lo + down * sin_hi


def _mix_in_kernel(x_ref, mod_ref, nw_ref, cos_ref, slo_ref, shi_ref,
                   w_qkv_ref, w_z_ref, w_ba_ref, w_q_ref, w_k_ref, w_v_ref, w_ga_ref, w_gb_ref,
                   qkv_ref, z_ref, ba_ref, q_ref, k_ref, v_ref, ga_ref, gb_ref):
    x = x_ref[0]
    h = _ada_rms(x, nw_ref[...], mod_ref[0, 3:4, :], mod_ref[0, 4:5, :]).astype(BF16)
    qkv_ref[0] = _dot(h, w_qkv_ref[...]).astype(BF16)
    z_ref[0] = _dot(h, w_z_ref[...]).astype(BF16)
    ba_ref[0] = _dot(h, w_ba_ref[...])
    v_ref[0] = _dot(h, w_v_ref[...]).astype(BF16)
    ga_ref[0] = _dot(h, w_ga_ref[...]).astype(BF16)
    gb_ref[0] = _dot(h, w_gb_ref[...]).astype(BF16)
    cos, slo, shi = cos_ref[...], slo_ref[...], shi_ref[...]
    q = _dot(h, w_q_ref[...])
    for hd in range(AT_HEADS):
        sl = slice(hd * LANES, (hd + 1) * LANES)
        q_ref[0, :, sl] = _rope(q[:, sl], cos, slo, shi).astype(BF16)
    k = _dot(h, w_k_ref[...])
    for hd in range(AT_KV_HEADS):
        sl = slice(hd * LANES, (hd + 1) * LANES)
        k_ref[0, :, sl] = _rope(k[:, sl], cos, slo, shi).astype(BF16)


def _mix_in_call(x, mod, norm_w, tabs, ws, *, tm):
    bsz, t, d = x.shape
    tok = lambda w, dt: (pl.BlockSpec((1, tm, w), lambda b, i: (b, i, 0)),
                         jax.ShapeDtypeStruct((bsz, t, w), dt))
    outs = [tok(DN_QKV_W, BF16), tok(DN_V_W, BF16), tok(LANES, F32), tok(AT_Q_W, BF16),
            tok(AT_KV_W, BF16), tok(AT_KV_W, BF16), tok(D_MODEL, BF16), tok(D_MODEL, BF16)]
    tab_spec = pl.BlockSpec((tm, LANES), lambda b, i: (i, 0))
    return pl.pallas_call(
        _mix_in_kernel,
        grid=(bsz, t // tm),
        in_specs=[
            pl.BlockSpec((1, tm, d), lambda b, i: (b, i, 0)),
            pl.BlockSpec((1, N_MOD, d), lambda b, i: (b, 0, 0)),
            _resident((1, d)),
            tab_spec, tab_spec, tab_spec,
        ] + [_resident(w.shape) for w in ws],
        out_specs=[o[0] for o in outs],
        out_shape=[o[1] for o in outs],
        name="mix_in",
        compiler_params=_params(("parallel", "parallel")),
    )(x, mod, norm_w, *tabs, *ws)


def _dn_prep_kernel(cur_ref, prev_ref, next_ref, ba_ref, cw_ref, alog_ref, dtb_ref,
                    qkv_ref, bg_ref, pad_ref, *, tb):
    i = pl.program_id(1)
    last = pl.num_programs(1) - 1
    halo = BF16_SUBLANES
    prev = jnp.where(i == 0, 0.0, prev_ref[0].astype(F32))
    nxt = jnp.where(i == last, 0.0, next_ref[0].astype(F32))
    pad_ref[0:halo, :] = prev
    pad_ref[halo:halo + tb, :] = cur_ref[0].astype(F32)
    pad_ref[halo + tb:, :] = nxt
    acc = None
    for j in range(DN_CONV):
        off = halo - DN_CONV // 2 + j
        term = pad_ref[off:off + tb, :] * cw_ref[j:j + 1, :]
        acc = term if acc is None else acc + term
    y = _silu(acc)
    for hd in range(2 * DN_HEADS):
        sl = slice(hd * LANES, (hd + 1) * LANES)
        yh = y[:, sl]
        n = yh * lax.rsqrt(jnp.sum(yh * yh, axis=-1, keepdims=True) + EPS)
        if hd < DN_HEADS:
            n = n * (DN_DK ** -0.5)
        qkv_ref[0, :, sl] = n.astype(BF16)
    qkv_ref[0, :, 2 * DN_QK_W:] = y[:, 2 * DN_QK_W:].astype(BF16)
    ba = ba_ref[0]
    lane = lax.broadcasted_iota(jnp.int32, ba.shape, 1)
    beta = _sigmoid(ba)
    z = ba + dtb_ref[...]
    softplus = jnp.maximum(z, 0.0) + jnp.log(1.0 + jnp.exp(-jnp.abs(z)))
    g = -jnp.exp(alog_ref[...]) * softplus
    nbg = 2 * DN_HEADS
    bg_ref[0] = jnp.where(lane < nbg, beta, jnp.where(lane < 2 * nbg, g, 0.0))


def _dn_prep_call(qkv, ba, conv_w, alog_row, dtb_row, *, tb):
    bsz, t, w = qkv.shape
    halo = BF16_SUBLANES
    r = tb // halo
    nh = t // halo
    return pl.pallas_call(
        functools.partial(_dn_prep_kernel, tb=tb),
        grid=(bsz, t // tb),
        in_specs=[
            pl.BlockSpec((1, tb, w), lambda b, i: (b, i, 0)),
            pl.BlockSpec((1, halo, w), lambda b, i: (b, jnp.maximum(i * r - 1, 0), 0)),
            pl.BlockSpec((1, halo, w), lambda b, i: (b, jnp.minimum((i + 1) * r, nh - 1), 0)),
            pl.BlockSpec((1, tb, LANES), lambda b, i: (b, i, 0)),
            _resident(conv_w.shape),
            _resident((1, LANES)),
            _resident((1, LANES)),
        ],
        out_specs=[
            pl.BlockSpec((1, tb, w), lambda b, i: (b, i, 0)),
            pl.BlockSpec((1, tb, LANES), lambda b, i: (b, i, 0)),
        ],
        out_shape=[jax.ShapeDtypeStruct(qkv.shape, BF16),
                   jax.ShapeDtypeStruct((bsz, t, LANES), F32)],
        scratch_shapes=[pltpu.VMEM((tb + 2 * halo, w), F32)],
        name="dn_prep",
        compiler_params=_params(("parallel", "parallel")),
    )(qkv, qkv, qkv, ba, conv_w, alog_row, dtb_row)


def _split2(a):
    hi = a.astype(BF16)
    lo = (a - hi.astype(F32)).astype(BF16)
    return hi, lo


def _mm_split(a, b):
    c = a.shape[0]
    n = b.shape[1]
    ah, al = _split2(a)
    bh, bl = _split2(b)
    p = _dot(jnp.concatenate([ah, al], axis=0), jnp.concatenate([bh, bl], axis=1))
    return p[:c, :n] + p[:c, n:] + p[c:, :n]


def _unit_lower_inverse(m_neg, eye):
    c = m_neg.shape[0]
    t = eye + m_neg
    p = _mm_split(m_neg, m_neg)
    span = 2
    while 2 * span < c:
        r = _mm_split(p, jnp.concatenate([t, p], axis=1))
        t = t + r[:, :c]
        p = r[:, c:]
        span *= 2
    return t + _mm_split(p, t)


def _dn_direction(qkv_ref, bg_ref, s_ref, o_ref, *, d, reverse):
    c = DN_CHUNK
    bg = bg_ref[0]
    row = lax.broadcasted_iota(jnp.int32, (c, c), 0)
    col = lax.broadcasted_iota(jnp.int32, (c, c), 1)
    if reverse:
        causal, strict = row <= col, row < col
        last = 0
    else:
        causal, strict = row >= col, row > col
        last = c - 1
    eye = jnp.where(row == col, 1.0, 0.0).astype(F32)
    lane = lax.broadcasted_iota(jnp.int32, bg.shape, 1)
    g = jnp.where(lane >= 2 * DN_HEADS, bg, 0.0)
    g1 = g.astype(BF16)
    r1 = g - g1.astype(F32)
    g2 = r1.astype(BF16)
    g3 = (r1 - g2.astype(F32)).astype(BF16)
    tri = jnp.where(causal, 1.0, 0.0).astype(BF16)
    gc3 = _dot(tri, jnp.concatenate([g1, g2, g3], axis=1))
    gc = gc3[:, :LANES] + gc3[:, LANES:2 * LANES] + gc3[:, 2 * LANES:]
    gct = gc.T
    e_gc = jnp.exp(gc)
    g_last = gc[last:last + 1, :]
    e_rest = jnp.exp(g_last - gc)
    e_last = jnp.exp(g_last)
    for hd in range(DN_HEADS):
        cb = d * DN_HEADS + hd
        cg = 2 * DN_HEADS + cb
        beta = bg[:, cb:cb + 1]
        decay = jnp.exp(jnp.where(causal, gc[:, cg:cg + 1] - gct[cg:cg + 1, :], -jnp.inf))
        eg = e_gc[:, cg:cg + 1]
        q = qkv_ref[0, :, hd * DN_DK:(hd + 1) * DN_DK]
        k = qkv_ref[0, :, DN_QK_W + hd * DN_DK:DN_QK_W + (hd + 1) * DN_DK]
        v = qkv_ref[0, :, 2 * DN_QK_W + hd * DN_DV:2 * DN_QK_W + (hd + 1) * DN_DV]
        gram = _dot_nt(jnp.concatenate([q, k], axis=0), k)
        qk = gram[:c] * decay
        m_neg = jnp.where(strict, -beta * gram[c:] * decay, 0.0)
        t_inv = _unit_lower_inverse(m_neg, eye)
        kf = k.astype(F32)
        rhs = jnp.concatenate([v.astype(F32) * beta, kf * (beta * eg)], axis=1).astype(BF16)
        sol = _dot(t_inv.astype(BF16), rhs)
        u0, w = sol[:, :DN_DV], sol[:, DN_DV:]
        q_dec = q.astype(F32) * eg
        s = s_ref[hd]
        ws = _dot(jnp.concatenate([w, q_dec], axis=0).astype(BF16), s.astype(BF16))
        u = u0 - ws[:c]
        ub = u.astype(BF16)
        o_ref[0, :, hd * DN_DV:(hd + 1) * DN_DV] = ws[c:] + _dot(qk.astype(BF16), ub)
        k_dec = (kf * e_rest[:, cg:cg + 1]).astype(BF16)
        s_ref[hd] = s * e_last[:, cg:cg + 1] + _dot_tn(k_dec, ub)


def _dn_scan_kernel(qkv_f_ref, bg_f_ref, qkv_b_ref, bg_b_ref, of_ref, ob_ref, sf_ref, sb_ref):
    @pl.when(pl.program_id(1) == 0)
    def _():
        sf_ref[...] = jnp.zeros_like(sf_ref)
        sb_ref[...] = jnp.zeros_like(sb_ref)

    _dn_direction(qkv_f_ref, bg_f_ref, sf_ref, of_ref, d=0, reverse=False)
    _dn_direction(qkv_b_ref, bg_b_ref, sb_ref, ob_ref, d=1, reverse=True)


def _dn_scan_call(qkv, bg):
    bsz, t, w = qkv.shape
    c = DN_CHUNK
    n = t // c
    fwd = lambda b, i: (b, i, 0)
    bwd = lambda b, i: (b, n - 1 - i, 0)
    return pl.pallas_call(
        _dn_scan_kernel,
        grid=(bsz, n),
        in_specs=[
            pl.BlockSpec((1, c, w), fwd),
            pl.BlockSpec((1, c, LANES), fwd),
            pl.BlockSpec((1, c, w), bwd),
            pl.BlockSpec((1, c, LANES), bwd),
        ],
        out_specs=[pl.BlockSpec((1, c, DN_V_W), fwd), pl.BlockSpec((1, c, DN_V_W), bwd)],
        out_shape=[jax.ShapeDtypeStruct((bsz, t, DN_V_W), F32)] * 2,
        scratch_shapes=[pltpu.VMEM((DN_HEADS, DN_DK, DN_DV), F32)] * 2,
        name="dn_scan",
        compiler_params=_params(("parallel", "arbitrary")),
    )(qkv, bg, qkv, bg)


def _attn_kernel(sink_ref, q_ref, kp_ref, kc_ref, kn_ref, vp_ref, vc_ref, vn_ref, o_ref, *, t):
    n = pl.program_id(1)
    wb = AT_WINDOW
    kw = jnp.concatenate([kp_ref[0], kc_ref[0], kn_ref[0]], axis=0)
    vw = jnp.concatenate([vp_ref[0], vc_ref[0], vn_ref[0]], axis=0)
    qi = lax.broadcasted_iota(jnp.int32, (wb, 3 * wb), 0)
    sj = lax.broadcasted_iota(jnp.int32, (wb, 3 * wb), 1)
    kpos = n * wb + sj - wb
    valid = (jnp.abs(sj - wb - qi) <= wb) & (kpos >= 0) & (kpos < t)
    scale = AT_HEAD_DIM ** -0.5
    for kv in range(AT_KV_HEADS):
        ksl = slice(kv * AT_HEAD_DIM, (kv + 1) * AT_HEAD_DIM)
        kh = kw[:, ksl]
        vh = vw[:, ksl]
        for g in range(AT_GROUP):
            hd = kv * AT_GROUP + g
            qsl = slice(hd * AT_HEAD_DIM, (hd + 1) * AT_HEAD_DIM)
            s = _dot_nt(q_ref[0, :, qsl], kh) * scale
            s = jnp.where(valid, s, -jnp.inf)
            sink = sink_ref[hd]
            m = jnp.maximum(jnp.max(s, axis=-1, keepdims=True), sink)
            p = jnp.exp(s - m)
            denom = jnp.sum(p, axis=-1, keepdims=True) + jnp.exp(sink - m)
            o = _dot(p.astype(BF16), vh) / denom
            o_ref[0, :, qsl] = o.astype(BF16)


def _attn_call(q, k, v, sink):
    bsz, t, _ = q.shape
    wb = AT_WINDOW
    nb = t // wb
    cur = lambda b, i: (b, i, 0)
    prev = lambda b, i: (b, jnp.maximum(i - 1, 0), 0)
    nxt = lambda b, i: (b, jnp.minimum(i + 1, nb - 1), 0)
    kv_spec = lambda f: pl.BlockSpec((1, wb, AT_KV_W), f)
    return pl.pallas_call(
        functools.partial(_attn_kernel, t=t),
        grid=(bsz, nb),
        in_specs=[
            pl.BlockSpec(memory_space=pltpu.SMEM),
            pl.BlockSpec((1, wb, AT_Q_W), cur),
            kv_spec(prev), kv_spec(cur), kv_spec(nxt),
            kv_spec(prev), kv_spec(cur), kv_spec(nxt),
        ],
        out_specs=pl.BlockSpec((1, wb, AT_Q_W), cur),
        out_shape=jax.ShapeDtypeStruct((bsz, t, AT_Q_W), BF16),
        name="attn",
        compiler_params=_params(("parallel", "parallel")),
    )(sink, q, k, k, k, v, v, v)


def _mix_out_kernel(x_ref, mod_ref, of_ref, ob_ref, z_ref, at_ref, ga_ref, gb_ref, dnw_ref,
                    wa_ref, wb_ref, wo_ref, o_ref, dn_ref):
    x = x_ref[0]
    o = of_ref[0] + ob_ref[0]
    dnw = dnw_ref[...]
    for hd in range(DN_HEADS):
        sl = slice(hd * DN_DV, (hd + 1) * DN_DV)
        y = _rms(o[:, sl], dnw) * _silu(z_ref[0, :, sl].astype(F32))
        dn_ref[:, sl] = y.astype(BF16)
    ya = _dot(dn_ref[...], wa_ref[...])
    yb = _dot(at_ref[0], wb_ref[...])
    merged = _sigmoid(ga_ref[0].astype(F32)) * ya + _sigmoid(gb_ref[0].astype(F32)) * yb
    gate = mod_ref[0, 5:6, :]
    o_ref[0] = x + gate * _dot(merged.astype(BF16), wo_ref[...])


def _mix_out_call(x, mod, o_f, o_b, z, at, ga, gb, dn_norm, wa, wb, wo, *, tm):
    bsz, t, d = x.shape
    tok = lambda w: pl.BlockSpec((1, tm, w), lambda b, i: (b, i, 0))
    return pl.pallas_call(
        _mix_out_kernel,
        grid=(bsz, t // tm),
        in_specs=[
            tok(d),
            pl.BlockSpec((1, N_MOD, d), lambda b, i: (b, 0, 0)),
            tok(DN_V_W), tok(DN_V_W), tok(DN_V_W), tok(AT_Q_W), tok(d), tok(d),
            _resident((1, DN_DV)),
            _resident(wa.shape), _resident(wb.shape), _resident(wo.shape),
        ],
        out_specs=tok(d),
        out_shape=jax.ShapeDtypeStruct(x.shape, F32),
        scratch_shapes=[pltpu.VMEM((tm, DN_V_W), BF16)],
        name="mix_out",
        compiler_params=_params(("parallel", "parallel")),
    )(x, mod, o_f, o_b, z, at, ga, gb, dn_norm, wa, wb, wo)


def _rope_tables(t):
    half = ROPE_DIM // 2
    inv = jnp.power(jnp.float32(ROPE_THETA), -jnp.arange(half, dtype=F32) / half)
    ang = jnp.arange(t, dtype=F32)[:, None] * inv[None, :]
    cos, sin = jnp.cos(ang), jnp.sin(ang)
    rest = LANES - ROPE_DIM
    cos_t = jnp.concatenate([cos, cos, jnp.ones((t, rest), F32)], axis=1)
    zeros = jnp.zeros((t, half), F32)
    sin_lo = jnp.concatenate([-sin, zeros, jnp.zeros((t, rest), F32)], axis=1)
    sin_hi = jnp.concatenate([zeros, sin, jnp.zeros((t, rest), F32)], axis=1)
    return cos_t, sin_lo, sin_hi


def _prepare_weights(w_ada, b_ada, ffn1_norm, ffn1_w_in, ffn1_w_out, mix_norm, w_in, conv_w, a_log,
                     dt_bias, dn_norm, attn_sink, w_proj_a, w_proj_b, w_out, ffn2_norm, ffn2_w_in,
                     ffn2_w_out, final_norm):
    row = lambda a: a.reshape(1, -1).astype(F32)
    offs = [0]
    for wdt in (DN_QKV_W, DN_V_W, 2 * DN_HEADS, 2 * DN_HEADS, AT_Q_W, AT_KV_W, AT_KV_W, D_MODEL, D_MODEL):
        offs.append(offs[-1] + wdt)
    seg = lambda i, j: w_in[:, offs[i]:offs[j]]
    w_ba = jnp.pad(seg(2, 4), ((0, 0), (0, LANES - 4 * DN_HEADS)))
    mix_ws = [seg(0, 1), seg(1, 2), w_ba, seg(4, 5), seg(5, 6), seg(6, 7), seg(7, 8), seg(8, 9)]
    lane_pad = lambda a: jnp.pad(a.reshape(1, -1).astype(F32),
                                 ((0, 0), (2 * DN_HEADS, LANES - 4 * DN_HEADS)))
    return dict(
        w_ada=w_ada.astype(BF16), b_ada=row(b_ada),
        ffn1=(row(ffn1_norm), ffn1_w_in[:, :D_FF].astype(BF16), ffn1_w_in[:, D_FF:].astype(BF16),
              ffn1_w_out.astype(BF16)),
        ffn2=(row(ffn2_norm), ffn2_w_in[:, :D_FF].astype(BF16), ffn2_w_in[:, D_FF:].astype(BF16),
              ffn2_w_out.astype(BF16)),
        mix_norm=row(mix_norm), mix_ws=[w.astype(BF16) for w in mix_ws],
        conv_w=conv_w.astype(F32), a_log=lane_pad(a_log), dt_bias=lane_pad(dt_bias),
        dn_norm=row(dn_norm), sink=attn_sink.astype(F32),
        w_proj_a=w_proj_a.astype(BF16), w_proj_b=w_proj_b.astype(BF16), w_out=w_out.astype(BF16),
        final_norm=row(final_norm),
    )


def _tile(t, want):
    tile = min(t, want)
    assert t % tile == 0, (t, tile)
    return tile


def _trunk(x, c, p):
    bsz, t, d = x.shape
    tm = _tile(t, 512)
    mod = _mod_call(c, p["w_ada"], p["b_ada"]).reshape(bsz, N_MOD, d)
    x = _ffn_call(x, mod, *p["ffn1"], p["final_norm"], sub=0, final=False, tm=tm)
    qkv, z, ba, q, k, v, ga, gb = _mix_in_call(x, mod, p["mix_norm"], _rope_tables(t), p["mix_ws"], tm=tm)
    qkv_n, bg = _dn_prep_call(qkv, ba, p["conv_w"], p["a_log"], p["dt_bias"], tb=_tile(t, 256))
    o_f, o_b = _dn_scan_call(qkv_n, bg)
    at = _attn_call(q, k, v, p["sink"])
    x = _mix_out_call(x, mod, o_f, o_b, z, at, ga, gb, p["dn_norm"], p["w_proj_a"], p["w_proj_b"],
                      p["w_out"], tm=tm)
    return _ffn_call(x, mod, *p["ffn2"], p["final_norm"], sub=2, final=True, tm=tm)


def kernel(x_prompt, x_sample, c_prompt, c_sample, w_ada, b_ada, ffn1_norm, ffn1_w_in, ffn1_w_out, mix_norm, w_in, conv_w, a_log, dt_bias, dn_norm, attn_sink, w_proj_a, w_proj_b, w_out, ffn2_norm, ffn2_w_in, ffn2_w_out, final_norm):
    depth = w_ada.shape[0]
    layers = [
        _prepare_weights(w_ada[l], b_ada[l], ffn1_norm[l], ffn1_w_in[l], ffn1_w_out[l], mix_norm[l],
                         w_in[l], conv_w[l], a_log[l], dt_bias[l], dn_norm[l], attn_sink[l],
                         w_proj_a[l], w_proj_b[l], w_out[l], ffn2_norm[l], ffn2_w_in[l],
                         ffn2_w_out[l], final_norm)
        for l in range(depth)
    ]
    assert depth == 1, "final RMSNorm is fused into the last layer's second FFN"
    y_prompt = _trunk(x_prompt, c_prompt, layers[0])
    y_sample = _trunk(x_sample, c_sample, layers[0])
    return (y_prompt, y_sample)
```

```python
import functools
import math

import jax
import jax.numpy as jnp
from jax import lax
from jax.experimental import pallas as pl
from jax.experimental.pallas import tpu as pltpu

D_MODEL = 1024
DN_HEADS = 8
DN_DK = 128
DN_DV = 128
DN_CONV = 5
DN_CHUNK = 64
AT_HEADS = 8
AT_KV_HEADS = 2
AT_HEAD_DIM = 128
AT_WINDOW = 128
ROPE_THETA = 500000.0
ROPE_DIM = AT_HEAD_DIM // 4
D_FF = 2816
EPS = 1e-6
N_MOD = 9

DN_QK_W = DN_HEADS * DN_DK
DN_V_W = DN_HEADS * DN_DV
DN_QKV_W = 2 * DN_QK_W + DN_V_W
AT_Q_W = AT_HEADS * AT_HEAD_DIM
AT_KV_W = AT_KV_HEADS * AT_HEAD_DIM
AT_GROUP = AT_HEADS // AT_KV_HEADS

LANES = 128
BF16_SUBLANES = 16
VMEM_LIMIT = 56 * 1024 * 1024

F32 = jnp.float32
BF16 = jnp.bfloat16


def _dot(a, b):
    return jnp.dot(a, b, preferred_element_type=F32)


def _dot_nt(a, b):
    return lax.dot_general(a, b, (((1,), (1,)), ((), ())), preferred_element_type=F32)


def _dot_tn(a, b):
    return lax.dot_general(a, b, (((0,), (0,)), ((), ())), preferred_element_type=F32)


def _rms(x, w):
    return x * lax.rsqrt(jnp.mean(x * x, axis=-1, keepdims=True) + EPS) * w


def _ada_rms(x, w, shift, scale):
    return _rms(x, w) * (1.0 + scale) + shift


def _sigmoid(x):
    return 1.0 / (1.0 + jnp.exp(-x))


def _silu(x):
    return x * _sigmoid(x)


def _resident(shape):
    nd = len(shape)
    return pl.BlockSpec(shape, lambda *_: (0,) * nd, pipeline_mode=pl.Buffered(1))


def _params(sem):
    return pltpu.CompilerParams(dimension_semantics=sem, vmem_limit_bytes=VMEM_LIMIT)


def _mod_kernel(c_ref, w_ref, b_ref, o_ref):
    c = c_ref[...]
    o_ref[...] = _dot(_silu(c).astype(BF16), w_ref[...]) + b_ref[...]


def _mod_call(c, w_ada, b_ada):
    bsz = c.shape[0]
    n = w_ada.shape[1]
    bn = n // N_MOD
    return pl.pallas_call(
        _mod_kernel,
        grid=(N_MOD,),
        in_specs=[
            pl.BlockSpec((bsz, D_MODEL), lambda j: (0, 0)),
            pl.BlockSpec((D_MODEL, bn), lambda j: (0, j)),
            pl.BlockSpec((1, bn), lambda j: (0, j)),
        ],
        out_specs=pl.BlockSpec((bsz, bn), lambda j: (0, j)),
        out_shape=jax.ShapeDtypeStruct((bsz, n), F32),
        name="mod",
        compiler_params=_params(("arbitrary",)),
    )(c, w_ada, b_ada)


def _ffn_kernel(x_ref, mod_ref, nw_ref, wg_ref, wu_ref, wo_ref, fw_ref, o_ref, *, sub, final):
    x = x_ref[0]
    shift = mod_ref[0, 3 * sub:3 * sub + 1, :]
    scale = mod_ref[0, 3 * sub + 1:3 * sub + 2, :]
    gate = mod_ref[0, 3 * sub + 2:3 * sub + 3, :]
    h = _ada_rms(x, nw_ref[...], shift, scale).astype(BF16)
    g = _dot(h, wg_ref[...])
    u = _dot(h, wu_ref[...])
    a = (_silu(g) * u).astype(BF16)
    y = x + 0.5 * gate * _dot(a, wo_ref[...])
    if final:
        y = _rms(y, fw_ref[...])
    o_ref[0] = y


def _ffn_call(x, mod, norm_w, wg, wu, wo, final_w, *, sub, final, tm):
    bsz, t, d = x.shape
    return pl.pallas_call(
        functools.partial(_ffn_kernel, sub=sub, final=final),
        grid=(bsz, t // tm),
        in_specs=[
            pl.BlockSpec((1, tm, d), lambda b, i: (b, i, 0)),
            pl.BlockSpec((1, N_MOD, d), lambda b, i: (b, 0, 0)),
            _resident((1, d)),
            _resident(wg.shape),
            _resident(wu.shape),
            _resident(wo.shape),
            _resident((1, d)),
        ],
        out_specs=pl.BlockSpec((1, tm, d), lambda b, i: (b, i, 0)),
        out_shape=jax.ShapeDtypeStruct(x.shape, F32),
        name="ffn",
        compiler_params=_params(("parallel", "parallel")),
    )(x, mod, norm_w, wg, wu, wo, final_w)


def _rope(x, cos, sin_lo, sin_hi):
    half = ROPE_DIM // 2
    up = pltpu.roll(x, LANES - half, 1)
    down = pltpu.roll(x, half, 1)
    return x * cos + up * sin_lo + down * sin_hi


def _mix_in_kernel(x_ref, mod_ref, nw_ref, cos_ref, slo_ref, shi_ref,
                   w_qkv_ref, w_z_ref, w_ba_ref, w_q_ref, w_k_ref, w_v_ref, w_ga_ref, w_gb_ref,
                   qkv_ref, z_ref, ba_ref, q_ref, k_ref, v_ref, ga_ref, gb_ref):
    x = x_ref[0]
    h = _ada_rms(x, nw_ref[...], mod_ref[0, 3:4, :], mod_ref[0, 4:5, :]).astype(BF16)
    qkv_ref[0] = _dot(h, w_qkv_ref[...]).astype(BF16)
    z_ref[0] = _dot(h, w_z_ref[...]).astype(BF16)
    ba_ref[0] = _dot(h, w_ba_ref[...])
    v_ref[0] = _dot(h, w_v_ref[...]).astype(BF16)
    ga_ref[0] = _dot(h, w_ga_ref[...]).astype(BF16)
    gb_ref[0] = _dot(h, w_gb_ref[...]).astype(BF16)
    cos, slo, shi = cos_ref[...], slo_ref[...], shi_ref[...]
    q = _dot(h, w_q_ref[...])
    for hd in range(AT_HEADS):
        sl = slice(hd * LANES, (hd + 1) * LANES)
        q_ref[0, :, sl] = _rope(q[:, sl], cos, slo, shi).astype(BF16)
    k = _dot(h, w_k_ref[...])
    for hd in range(AT_KV_HEADS):
        sl = slice(hd * LANES, (hd + 1) * LANES)
        k_ref[0, :, sl] = _rope(k[:, sl], cos, slo, shi).astype(BF16)


def _mix_in_call(x, mod, norm_w, tabs, ws, *, tm):
    bsz, t, d = x.shape
    tok = lambda w, dt: (pl.BlockSpec((1, tm, w), lambda b, i: (b, i, 0)),
                         jax.ShapeDtypeStruct((bsz, t, w), dt))
    outs = [tok(DN_QKV_W, BF16), tok(DN_V_W, BF16), tok(LANES, F32), tok(AT_Q_W, BF16),
            tok(AT_KV_W, BF16), tok(AT_KV_W, BF16), tok(D_MODEL, BF16), tok(D_MODEL, BF16)]
    tab_spec = pl.BlockSpec((tm, LANES), lambda b, i: (i, 0))
    return pl.pallas_call(
        _mix_in_kernel,
        grid=(bsz, t // tm),
        in_specs=[
            pl.BlockSpec((1, tm, d), lambda b, i: (b, i, 0)),
            pl.BlockSpec((1, N_MOD, d), lambda b, i: (b, 0, 0)),
            _resident((1, d)),
            tab_spec, tab_spec, tab_spec,
        ] + [_resident(w.shape) for w in ws],
        out_specs=[o[0] for o in outs],
        out_shape=[o[1] for o in outs],
        name="mix_in",
        compiler_params=_params(("parallel", "parallel")),
    )(x, mod, norm_w, *tabs, *ws)


def _dn_prep_kernel(cur_ref, prev_ref, next_ref, ba_ref, cw_ref, alog_ref, dtb_ref,
                    qkv_ref, bg_ref, pad_ref, *, tb):
    i = pl.program_id(1)
    last = pl.num_programs(1) - 1
    halo = BF16_SUBLANES
    prev = jnp.where(i == 0, 0.0, prev_ref[0].astype(F32))
    nxt = jnp.where(i == last, 0.0, next_ref[0].astype(F32))
    pad_ref[0:halo, :] = prev
    pad_ref[halo:halo + tb, :] = cur_ref[0].astype(F32)
    pad_ref[halo + tb:, :] = nxt
    acc = None
    for j in range(DN_CONV):
        off = halo - DN_CONV // 2 + j
        term = pad_ref[off:off + tb, :] * cw_ref[j:j + 1, :]
        acc = term if acc is None else acc + term
    y = _silu(acc)
    for hd in range(2 * DN_HEADS):
        sl = slice(hd * LANES, (hd + 1) * LANES)
        yh = y[:, sl]
        n = yh * lax.rsqrt(jnp.sum(yh * yh, axis=-1, keepdims=True) + EPS)
        if hd < DN_HEADS:
            n = n * (DN_DK ** -0.5)
        qkv_ref[0, :, sl] = n.astype(BF16)
    qkv_ref[0, :, 2 * DN_QK_W:] = y[:, 2 * DN_QK_W:].astype(BF16)
    ba = ba_ref[0]
    lane = lax.broadcasted_iota(jnp.int32, ba.shape, 1)
    beta = _sigmoid(ba)
    z = ba + dtb_ref[...]
    softplus = jnp.maximum(z, 0.0) + jnp.log(1.0 + jnp.exp(-jnp.abs(z)))
    g = -jnp.exp(alog_ref[...]) * softplus
    nbg = 2 * DN_HEADS
    bg_ref[0] = jnp.where(lane < nbg, beta, jnp.where(lane < 2 * nbg, g, 0.0))


def _dn_prep_call(qkv, ba, conv_w, alog_row, dtb_row, *, tb):
    bsz, t, w = qkv.shape
    halo = BF16_SUBLANES
    r = tb // halo
    nh = t // halo
    return pl.pallas_call(
        functools.partial(_dn_prep_kernel, tb=tb),
        grid=(bsz, t // tb),
        in_specs=[
            pl.BlockSpec((1, tb, w), lambda b, i: (b, i, 0)),
            pl.BlockSpec((1, halo, w), lambda b, i: (b, jnp.maximum(i * r - 1, 0), 0)),
            pl.BlockSpec((1, halo, w), lambda b, i: (b, jnp.minimum((i + 1) * r, nh - 1), 0)),
            pl.BlockSpec((1, tb, LANES), lambda b, i: (b, i, 0)),
            _resident(conv_w.shape),
            _resident((1, LANES)),
            _resident((1, LANES)),
        ],
        out_specs=[
            pl.BlockSpec((1, tb, w), lambda b, i: (b, i, 0)),
            pl.BlockSpec((1, tb, LANES), lambda b, i: (b, i, 0)),
        ],
        out_shape=[jax.ShapeDtypeStruct(qkv.shape, BF16),
                   jax.ShapeDtypeStruct((bsz, t, LANES), F32)],
        scratch_shapes=[pltpu.VMEM((tb + 2 * halo, w), F32)],
        name="dn_prep",
        compiler_params=_params(("parallel", "parallel")),
    )(qkv, qkv, qkv, ba, conv_w, alog_row, dtb_row)


def _split2(a):
    hi = a.astype(BF16)
    lo = (a - hi.astype(F32)).astype(BF16)
    return hi, lo


def _dn_direction(qkv_ref, bg_ref, s_ref, o_ref, *, d, reverse):
    c = DN_CHUNK
    heads = range(DN_HEADS)
    bg = bg_ref[0]
    row = lax.broadcasted_iota(jnp.int32, (c, 2 * c), 0)
    lane2 = lax.broadcasted_iota(jnp.int32, (c, 2 * c), 1)
    t_part = lane2 < c
    col = jnp.where(t_part, lane2, lane2 - c)
    if reverse:
        causal, strict = row <= col, row < col
        last = 0
    else:
        causal, strict = row >= col, row > col
        last = c - 1
    eye_t = jnp.where(t_part, jnp.where(row == col, 1.0, 0.0), 0.0).astype(F32)
    strict_p = strict & (lane2 >= c)
    lane = lax.broadcasted_iota(jnp.int32, bg.shape, 1)
    g = jnp.where(lane >= 2 * DN_HEADS, bg, 0.0)
    g1 = g.astype(BF16)
    r1 = g - g1.astype(F32)
    g2 = r1.astype(BF16)
    g3 = (r1 - g2.astype(F32)).astype(BF16)
    tri = jnp.where(causal[:, :c], 1.0, 0.0).astype(BF16)
    gc3 = _dot(tri, jnp.concatenate([g1, g2, g3], axis=1))
    gc = gc3[:, :LANES] + gc3[:, LANES:2 * LANES] + gc3[:, 2 * LANES:]
    gct = gc.T
    gct2 = jnp.concatenate([gct, gct], axis=1)
    e_gc = jnp.exp(gc)
    g_last = gc[last:last + 1, :]
    e_rest = jnp.exp(g_last - gc)
    e_last = jnp.exp(g_last)
    g_lane = lambda hd: 2 * DN_HEADS + d * DN_HEADS + hd
    col_of = lambda a, hd: a[:, g_lane(hd):g_lane(hd) + 1]
    beta = [bg[:, d * DN_HEADS + hd:d * DN_HEADS + hd + 1] for hd in heads]
    load = lambda base, hd: qkv_ref[0, :, base + hd * DN_DK:base + (hd + 1) * DN_DK]

    qk, w = [], []
    for hd in heads:
        q, k = load(0, hd), load(DN_QK_W, hd)
        gram = _dot_nt(jnp.concatenate([q, k], axis=0), jnp.concatenate([k, k], axis=0))
        diff = col_of(gc, hd) - gct2[g_lane(hd):g_lane(hd) + 1, :]
        decay = jnp.exp(jnp.where(causal, diff, -jnp.inf))
        qk.append((gram[:c] * decay)[:, :c].astype(BF16))
        w.append(eye_t + jnp.where(strict_p, -beta[hd] * gram[c:] * decay, 0.0))

    zeros = jnp.zeros((c, 4 * c), BF16)
    levels = c.bit_length() - 1
    assert 1 << levels == c
    for _ in range(levels):
        for hd in heads:
            wh, wl = _split2(w[hd])
            p = _dot(jnp.concatenate([wh, wl], axis=0),
                     jnp.concatenate([zeros, jnp.concatenate([wh, wl], axis=1)], axis=0))
            r = p[:c, :2 * c] + p[:c, 2 * c:] + p[c:, :2 * c]
            w[hd] = jnp.where(t_part, w[hd] + r, r)

    sol = []
    for hd in heads:
        kf = load(DN_QK_W, hd).astype(F32)
        vf = load(2 * DN_QK_W, hd).astype(F32)
        rhs = jnp.concatenate([vf * beta[hd], kf * (beta[hd] * col_of(e_gc, hd))], axis=1)
        sol.append(_dot(w[hd][:, :c].astype(BF16), rhs.astype(BF16)))

    for hd in heads:
        u0, wmat = sol[hd][:, :DN_DV], sol[hd][:, DN_DV:]
        q_dec = load(0, hd).astype(F32) * col_of(e_gc, hd)
        s = s_ref[hd]
        ws = _dot(jnp.concatenate([wmat, q_dec], axis=0).astype(BF16), s.astype(BF16))
        ub = (u0 - ws[:c]).astype(BF16)
        o_ref[0, :, hd * DN_DV:(hd + 1) * DN_DV] = ws[c:] + _dot(qk[hd], ub)
        k_dec = (load(DN_QK_W, hd).astype(F32) * col_of(e_rest, hd)).astype(BF16)
        s_ref[hd] = s * col_of(e_last, hd) + _dot_tn(k_dec, ub)


def _dn_scan_kernel(qkv_f_ref, bg_f_ref, qkv_b_ref, bg_b_ref, of_ref, ob_ref, sf_ref, sb_ref):
    @pl.when(pl.program_id(1) == 0)
    def _():
        sf_ref[...] = jnp.zeros_like(sf_ref)
        sb_ref[...] = jnp.zeros_like(sb_ref)

    _dn_direction(qkv_f_ref, bg_f_ref, sf_ref, of_ref, d=0, reverse=False)
    _dn_direction(qkv_b_ref, bg_b_ref, sb_ref, ob_ref, d=1, reverse=True)


def _dn_scan_call(qkv, bg):
    bsz, t, w = qkv.shape
    c = DN_CHUNK
    n = t // c
    fwd = lambda b, i: (b, i, 0)
    bwd = lambda b, i: (b, n - 1 - i, 0)
    return pl.pallas_call(
        _dn_scan_kernel,
        grid=(bsz, n),
        in_specs=[
            pl.BlockSpec((1, c, w), fwd),
            pl.BlockSpec((1, c, LANES), fwd),
            pl.BlockSpec((1, c, w), bwd),
            pl.BlockSpec((1, c, LANES), bwd),
        ],
        out_specs=[pl.BlockSpec((1, c, DN_V_W), fwd), pl.BlockSpec((1, c, DN_V_W), bwd)],
        out_shape=[jax.ShapeDtypeStruct((bsz, t, DN_V_W), F32)] * 2,
        scratch_shapes=[pltpu.VMEM((DN_HEADS, DN_DK, DN_DV), F32)] * 2,
        name="dn_scan",
        compiler_params=_params(("parallel", "arbitrary")),
    )(qkv, bg, qkv, bg)


def _attn_kernel(sink_ref, q_ref, kp_ref, kc_ref, kn_ref, vp_ref, vc_ref, vn_ref, o_ref, *, t):
    n = pl.program_id(1)
    wb = AT_WINDOW
    kw = jnp.concatenate([kp_ref[0], kc_ref[0], kn_ref[0]], axis=0)
    vw = jnp.concatenate([vp_ref[0], vc_ref[0], vn_ref[0]], axis=0)
    qi = lax.broadcasted_iota(jnp.int32, (wb, 3 * wb), 0)
    sj = lax.broadcasted_iota(jnp.int32, (wb, 3 * wb), 1)
    kpos = n * wb + sj - wb
    valid = (jnp.abs(sj - wb - qi) <= wb) & (kpos >= 0) & (kpos < t)
    scale = AT_HEAD_DIM ** -0.5
    for kv in range(AT_KV_HEADS):
        ksl = slice(kv * AT_HEAD_DIM, (kv + 1) * AT_HEAD_DIM)
        kh = kw[:, ksl]
        vh = vw[:, ksl]
        for g in range(AT_GROUP):
            hd = kv * AT_GROUP + g
            qsl = slice(hd * AT_HEAD_DIM, (hd + 1) * AT_HEAD_DIM)
            s = _dot_nt(q_ref[0, :, qsl], kh) * scale
            s = jnp.where(valid, s, -jnp.inf)
            sink = sink_ref[hd]
            m = jnp.maximum(jnp.max(s, axis=-1, keepdims=True), sink)
            p = jnp.exp(s - m)
            denom = jnp.sum(p, axis=-1, keepdims=True) + jnp.exp(sink - m)
            o = _dot(p.astype(BF16), vh) / denom
            o_ref[0, :, qsl] = o.astype(BF16)


def _attn_call(q, k, v, sink):
    bsz, t, _ = q.shape
    wb = AT_WINDOW
    nb = t // wb
    cur = lambda b, i: (b, i, 0)
    prev = lambda b, i: (b, jnp.maximum(i - 1, 0), 0)
    nxt = lambda b, i: (b, jnp.minimum(i + 1, nb - 1), 0)
    kv_spec = lambda f: pl.BlockSpec((1, wb, AT_KV_W), f)
    return pl.pallas_call(
        functools.partial(_attn_kernel, t=t),
        grid=(bsz, nb),
        in_specs=[
            pl.BlockSpec(memory_space=pltpu.SMEM),
            pl.BlockSpec((1, wb, AT_Q_W), cur),
            kv_spec(prev), kv_spec(cur), kv_spec(nxt),
            kv_spec(prev), kv_spec(cur), kv_spec(nxt),
        ],
        out_specs=pl.BlockSpec((1, wb, AT_Q_W), cur),
        out_shape=jax.ShapeDtypeStruct((bsz, t, AT_Q_W), BF16),
        name="attn",
        compiler_params=_params(("parallel", "parallel")),
    )(sink, q, k, k, k, v, v, v)


def _mix_out_kernel(x_ref, mod_ref, of_ref, ob_ref, z_ref, at_ref, ga_ref, gb_ref, dnw_ref,
                    wa_ref, wb_ref, wo_ref, o_ref, dn_ref):
    x = x_ref[0]
    o = of_ref[0] + ob_ref[0]
    dnw = dnw_ref[...]
    for hd in range(DN_HEADS):
        sl = slice(hd * DN_DV, (hd + 1) * DN_DV)
        y = _rms(o[:, sl], dnw) * _silu(z_ref[0, :, sl].astype(F32))
        dn_ref[:, sl] = y.astype(BF16)
    ya = _dot(dn_ref[...], wa_ref[...])
    yb = _dot(at_ref[0], wb_ref[...])
    merged = _sigmoid(ga_ref[0].astype(F32)) * ya + _sigmoid(gb_ref[0].astype(F32)) * yb
    gate = mod_ref[0, 5:6, :]
    o_ref[0] = x + gate * _dot(merged.astype(BF16), wo_ref[...])


def _mix_out_call(x, mod, o_f, o_b, z, at, ga, gb, dn_norm, wa, wb, wo, *, tm):
    bsz, t, d = x.shape
    tok = lambda w: pl.BlockSpec((1, tm, w), lambda b, i: (b, i, 0))
    return pl.pallas_call(
        _mix_out_kernel,
        grid=(bsz, t // tm),
        in_specs=[
            tok(d),
            pl.BlockSpec((1, N_MOD, d), lambda b, i: (b, 0, 0)),
            tok(DN_V_W), tok(DN_V_W), tok(DN_V_W), tok(AT_Q_W), tok(d), tok(d),
            _resident((1, DN_DV)),
            _resident(wa.shape), _resident(wb.shape), _resident(wo.shape),
        ],
        out_specs=tok(d),
        out_shape=jax.ShapeDtypeStruct(x.shape, F32),
        scratch_shapes=[pltpu.VMEM((tm, DN_V_W), BF16)],
        name="mix_out",
        compiler_params=_params(("parallel", "parallel")),
    )(x, mod, o_f, o_b, z, at, ga, gb, dn_norm, wa, wb, wo)


def _rope_tables(t):
    half = ROPE_DIM // 2
    inv = jnp.power(jnp.float32(ROPE_THETA), -jnp.arange(half, dtype=F32) / half)
    ang = jnp.arange(t, dtype=F32)[:, None] * inv[None, :]
    cos, sin = jnp.cos(ang), jnp.sin(ang)
    rest = LANES - ROPE_DIM
    cos_t = jnp.concatenate([cos, cos, jnp.ones((t, rest), F32)], axis=1)
    zeros = jnp.zeros((t, half), F32)
    sin_lo = jnp.concatenate([-sin, zeros, jnp.zeros((t, rest), F32)], axis=1)
    sin_hi = jnp.concatenate([zeros, sin, jnp.zeros((t, rest), F32)], axis=1)
    return cos_t, sin_lo, sin_hi


def _prepare_weights(w_ada, b_ada, ffn1_norm, ffn1_w_in, ffn1_w_out, mix_norm, w_in, conv_w, a_log,
                     dt_bias, dn_norm, attn_sink, w_proj_a, w_proj_b, w_out, ffn2_norm, ffn2_w_in,
                     ffn2_w_out, final_norm):
    row = lambda a: a.reshape(1, -1).astype(F32)
    offs = [0]
    for wdt in (DN_QKV_W, DN_V_W, 2 * DN_HEADS, 2 * DN_HEADS, AT_Q_W, AT_KV_W, AT_KV_W, D_MODEL, D_MODEL):
        offs.append(offs[-1] + wdt)
    seg = lambda i, j: w_in[:, offs[i]:offs[j]]
    w_ba = jnp.pad(seg(2, 4), ((0, 0), (0, LANES - 4 * DN_HEADS)))
    mix_ws = [seg(0, 1), seg(1, 2), w_ba, seg(4, 5), seg(5, 6), seg(6, 7), seg(7, 8), seg(8, 9)]
    lane_pad = lambda a: jnp.pad(a.reshape(1, -1).astype(F32),
                                 ((0, 0), (2 * DN_HEADS, LANES - 4 * DN_HEADS)))
    return dict(
        w_ada=w_ada.astype(BF16), b_ada=row(b_ada),
        ffn1=(row(ffn1_norm), ffn1_w_in[:, :D_FF].astype(BF16), ffn1_w_in[:, D_FF:].astype(BF16),
              ffn1_w_out.astype(BF16)),
        ffn2=(row(ffn2_norm), ffn2_w_in[:, :D_FF].astype(BF16), ffn2_w_in[:, D_FF:].astype(BF16),
              ffn2_w_out.astype(BF16)),
        mix_norm=row(mix_norm), mix_ws=[w.astype(BF16) for w in mix_ws],
        conv_w=conv_w.astype(F32), a_log=lane_pad(a_log), dt_bias=lane_pad(dt_bias),
        dn_norm=row(dn_norm), sink=attn_sink.astype(F32),
        w_proj_a=w_proj_a.astype(BF16), w_proj_b=w_proj_b.astype(BF16), w_out=w_out.astype(BF16),
        final_norm=row(final_norm),
    )


def _tile(t, want):
    tile = min(t, want)
    assert t % tile == 0, (t, tile)
    return tile


def _trunk(x, c, p):
    bsz, t, d = x.shape
    tm = _tile(t, 512)
    mod = _mod_call(c, p["w_ada"], p["b_ada"]).reshape(bsz, N_MOD, d)
    x = _ffn_call(x, mod, *p["ffn1"], p["final_norm"], sub=0, final=False, tm=tm)
    qkv, z, ba, q, k, v, ga, gb = _mix_in_call(x, mod, p["mix_norm"], _rope_tables(t), p["mix_ws"], tm=tm)
    qkv_n, bg = _dn_prep_call(qkv, ba, p["conv_w"], p["a_log"], p["dt_bias"], tb=_tile(t, 256))
    o_f, o_b = _dn_scan_call(qkv_n, bg)
    at = _attn_call(q, k, v, p["sink"])
    x = _mix_out_call(x, mod, o_f, o_b, z, at, ga, gb, p["dn_norm"], p["w_proj_a"], p["w_proj_b"],
                      p["w_out"], tm=tm)
    return _ffn_call(x, mod, *p["ffn2"], p["final_norm"], sub=2, final=True, tm=tm)


def kernel(x_prompt, x_sample, c_prompt, c_sample, w_ada, b_ada, ffn1_norm, ffn1_w_in, ffn1_w_out, mix_norm, w_in, conv_w, a_log, dt_bias, dn_norm, attn_sink, w_proj_a, w_proj_b, w_out, ffn2_norm, ffn2_w_in, ffn2_w_out, final_norm):
    depth = w_ada.shape[0]
    layers = [
        _prepare_weights(w_ada[l], b_ada[l], ffn1_norm[l], ffn1_w_in[l], ffn1_w_out[l], mix_norm[l],
                         w_in[l], conv_w[l], a_log[l], dt_bias[l], dn_norm[l], attn_sink[l],
                         w_proj_a[l], w_proj_b[l], w_out[l], ffn2_norm[l], ffn2_w_in[l],
                         ffn2_w_out[l], final_norm)
        for l in range(depth)
    ]
    assert depth == 1, "final RMSNorm is fused into the last layer's second FFN"
    y_prompt = _trunk(x_prompt, c_prompt, layers[0])
    y_sample = _trunk(x_sample, c_sample, layers[0])
    return (y_prompt, y_sample)
```

```python
import functools
import types

import jax
import jax.numpy as jnp
from jax import lax
from jax.experimental import pallas as pl
from jax.experimental.pallas import tpu as pltpu

D_MODEL = 1024
DN_HEADS = 8
DN_DK = 128
DN_DV = 128
DN_CONV = 5
DN_CHUNK = 64
AT_HEADS = 8
AT_KV_HEADS = 2
AT_HEAD_DIM = 128
AT_WINDOW = 128
ROPE_THETA = 500000.0
ROPE_DIM = AT_HEAD_DIM // 4
D_FF = 2816
EPS = 1e-6
N_MOD = 9

DN_QK_W = DN_HEADS * DN_DK
DN_V_W = DN_HEADS * DN_DV
DN_QKV_W = 2 * DN_QK_W + DN_V_W
AT_Q_W = AT_HEADS * AT_HEAD_DIM
AT_KV_W = AT_KV_HEADS * AT_HEAD_DIM
AT_GROUP = AT_HEADS // AT_KV_HEADS

LANES = 128
BF16_SUBLANES = 16
VMEM_LIMIT = 56 * 1024 * 1024

F32 = jnp.float32
BF16 = jnp.bfloat16


def _dot(a, b):
    return jnp.dot(a, b, preferred_element_type=F32)


def _dot_nt(a, b):
    return lax.dot_general(a, b, (((1,), (1,)), ((), ())), preferred_element_type=F32)


def _dot_tn(a, b):
    return lax.dot_general(a, b, (((0,), (0,)), ((), ())), preferred_element_type=F32)


def _rms(x, w):
    return x * lax.rsqrt(jnp.mean(x * x, axis=-1, keepdims=True) + EPS) * w


def _ada_rms(x, w, shift, scale):
    return _rms(x, w) * (1.0 + scale) + shift


def _sigmoid(x):
    return 1.0 / (1.0 + jnp.exp(-x))


def _silu(x):
    return x * _sigmoid(x)


def _resident(shape):
    nd = len(shape)
    return pl.BlockSpec(shape, lambda *_: (0,) * nd, pipeline_mode=pl.Buffered(1))


def _params(sem):
    return pltpu.CompilerParams(dimension_semantics=sem, vmem_limit_bytes=VMEM_LIMIT)


def _mod_kernel(c_ref, w_ref, b_ref, o_ref):
    c = c_ref[...]
    o_ref[...] = _dot(_silu(c).astype(BF16), w_ref[...]) + b_ref[...]


def _mod_call(c, w_ada, b_ada):
    bsz = c.shape[0]
    n = w_ada.shape[1]
    bn = n // N_MOD
    return pl.pallas_call(
        _mod_kernel,
        grid=(N_MOD,),
        in_specs=[
            pl.BlockSpec((bsz, D_MODEL), lambda j: (0, 0)),
            pl.BlockSpec((D_MODEL, bn), lambda j: (0, j)),
            pl.BlockSpec((1, bn), lambda j: (0, j)),
        ],
        out_specs=pl.BlockSpec((bsz, bn), lambda j: (0, j)),
        out_shape=jax.ShapeDtypeStruct((bsz, n), F32),
        name="mod",
        compiler_params=_params(("arbitrary",)),
    )(c, w_ada, b_ada)


def _ffn_kernel(x_ref, mod_ref, nw_ref, wg_ref, wu_ref, wo_ref, fw_ref, o_ref, *, sub, final):
    x = x_ref[0]
    shift = mod_ref[0, 3 * sub:3 * sub + 1, :]
    scale = mod_ref[0, 3 * sub + 1:3 * sub + 2, :]
    gate = mod_ref[0, 3 * sub + 2:3 * sub + 3, :]
    h = _ada_rms(x, nw_ref[...], shift, scale).astype(BF16)
    g = _dot(h, wg_ref[...])
    u = _dot(h, wu_ref[...])
    a = (_silu(g) * u).astype(BF16)
    y = x + 0.5 * gate * _dot(a, wo_ref[...])
    if final:
        y = _rms(y, fw_ref[...])
    o_ref[0] = y


def _ffn_call(x, mod, norm_w, wg, wu, wo, final_w, *, sub, final, tm):
    bsz, t, d = x.shape
    return pl.pallas_call(
        functools.partial(_ffn_kernel, sub=sub, final=final),
        grid=(bsz, t // tm),
        in_specs=[
            pl.BlockSpec((1, tm, d), lambda b, i: (b, i, 0)),
            pl.BlockSpec((1, N_MOD, d), lambda b, i: (b, 0, 0)),
            _resident((1, d)),
            _resident(wg.shape),
            _resident(wu.shape),
            _resident(wo.shape),
            _resident((1, d)),
        ],
        out_specs=pl.BlockSpec((1, tm, d), lambda b, i: (b, i, 0)),
        out_shape=jax.ShapeDtypeStruct(x.shape, F32),
        name="ffn",
        compiler_params=_params(("parallel", "parallel")),
    )(x, mod, norm_w, wg, wu, wo, final_w)


def _rope(x, cos, sin_lo, sin_hi):
    half = ROPE_DIM // 2
    up = pltpu.roll(x, LANES - half, 1)
    down = pltpu.roll(x, half, 1)
    return x * cos + up * sin_lo + down * sin_hi


def _mix_in_kernel(x_ref, mod_ref, nw_ref, cos_ref, slo_ref, shi_ref,
                   w_qkv_ref, w_z_ref, w_ba_ref, w_q_ref, w_k_ref, w_v_ref, w_ga_ref, w_gb_ref,
                   qkv_ref, z_ref, ba_ref, q_ref, k_ref, v_ref, ga_ref, gb_ref):
    x = x_ref[0]
    h = _ada_rms(x, nw_ref[...], mod_ref[0, 3:4, :], mod_ref[0, 4:5, :]).astype(BF16)
    qkv_ref[0] = _dot(h, w_qkv_ref[...]).astype(BF16)
    z_ref[0] = _dot(h, w_z_ref[...]).astype(BF16)
    ba_ref[0] = _dot(h, w_ba_ref[...])
    v_ref[0] = _dot(h, w_v_ref[...]).astype(BF16)
    ga_ref[0] = _dot(h, w_ga_ref[...]).astype(BF16)
    gb_ref[0] = _dot(h, w_gb_ref[...]).astype(BF16)
    cos, slo, shi = cos_ref[...], slo_ref[...], shi_ref[...]
    q = _dot(h, w_q_ref[...])
    for hd in range(AT_HEADS):
        sl = slice(hd * LANES, (hd + 1) * LANES)
        q_ref[0, :, sl] = _rope(q[:, sl], cos, slo, shi).astype(BF16)
    k = _dot(h, w_k_ref[...])
    for hd in range(AT_KV_HEADS):
        sl = slice(hd * LANES, (hd + 1) * LANES)
        k_ref[0, :, sl] = _rope(k[:, sl], cos, slo, shi).astype(BF16)


def _mix_in_call(x, mod, norm_w, tabs, ws, *, tm):
    bsz, t, d = x.shape
    tok = lambda w, dt: (pl.BlockSpec((1, tm, w), lambda b, i: (b, i, 0)),
                         jax.ShapeDtypeStruct((bsz, t, w), dt))
    outs = [tok(DN_QKV_W, BF16), tok(DN_V_W, BF16), tok(LANES, F32), tok(AT_Q_W, BF16),
            tok(AT_KV_W, BF16), tok(AT_KV_W, BF16), tok(D_MODEL, BF16), tok(D_MODEL, BF16)]
    tab_spec = pl.BlockSpec((tm, LANES), lambda b, i: (i, 0))
    return pl.pallas_call(
        _mix_in_kernel,
        grid=(bsz, t // tm),
        in_specs=[
            pl.BlockSpec((1, tm, d), lambda b, i: (b, i, 0)),
            pl.BlockSpec((1, N_MOD, d), lambda b, i: (b, 0, 0)),
            _resident((1, d)),
            tab_spec, tab_spec, tab_spec,
        ] + [_resident(w.shape) for w in ws],
        out_specs=[o[0] for o in outs],
        out_shape=[o[1] for o in outs],
        name="mix_in",
        compiler_params=_params(("parallel", "parallel")),
    )(x, mod, norm_w, *tabs, *ws)


def _dn_prep_kernel(cur_ref, prev_ref, next_ref, ba_ref, cw_ref, alog_ref, dtb_ref,
                    qkv_ref, bg_ref, pad_ref, *, tb):
    i = pl.program_id(1)
    last = pl.num_programs(1) - 1
    halo = BF16_SUBLANES
    prev = jnp.where(i == 0, 0.0, prev_ref[0].astype(F32))
    nxt = jnp.where(i == last, 0.0, next_ref[0].astype(F32))
    pad_ref[0:halo, :] = prev
    pad_ref[halo:halo + tb, :] = cur_ref[0].astype(F32)
    pad_ref[halo + tb:, :] = nxt
    acc = None
    for j in range(DN_CONV):
        off = halo - DN_CONV // 2 + j
        term = pad_ref[off:off + tb, :] * cw_ref[j:j + 1, :]
        acc = term if acc is None else acc + term
    y = _silu(acc)
    for hd in range(2 * DN_HEADS):
        sl = slice(hd * LANES, (hd + 1) * LANES)
        yh = y[:, sl]
        n = yh * lax.rsqrt(jnp.sum(yh * yh, axis=-1, keepdims=True) + EPS)
        if hd < DN_HEADS:
            n = n * (DN_DK ** -0.5)
        qkv_ref[0, :, sl] = n.astype(BF16)
    qkv_ref[0, :, 2 * DN_QK_W:] = y[:, 2 * DN_QK_W:].astype(BF16)
    ba = ba_ref[0]
    lane = lax.broadcasted_iota(jnp.int32, ba.shape, 1)
    beta = _sigmoid(ba)
    z = ba + dtb_ref[...]
    softplus = jnp.maximum(z, 0.0) + jnp.log(1.0 + jnp.exp(-jnp.abs(z)))
    g = -jnp.exp(alog_ref[...]) * softplus
    nbg = 2 * DN_HEADS
    bg_ref[0] = jnp.where(lane < nbg, beta, jnp.where(lane < 2 * nbg, g, 0.0))


def _dn_prep_call(qkv, ba, conv_w, alog_row, dtb_row, *, tb):
    bsz, t, w = qkv.shape
    halo = BF16_SUBLANES
    r = tb // halo
    nh = t // halo
    return pl.pallas_call(
        functools.partial(_dn_prep_kernel, tb=tb),
        grid=(bsz, t // tb),
        in_specs=[
            pl.BlockSpec((1, tb, w), lambda b, i: (b, i, 0)),
            pl.BlockSpec((1, halo, w), lambda b, i: (b, jnp.maximum(i * r - 1, 0), 0)),
            pl.BlockSpec((1, halo, w), lambda b, i: (b, jnp.minimum((i + 1) * r, nh - 1), 0)),
            pl.BlockSpec((1, tb, LANES), lambda b, i: (b, i, 0)),
            _resident(conv_w.shape),
            _resident((1, LANES)),
            _resident((1, LANES)),
        ],
        out_specs=[
            pl.BlockSpec((1, tb, w), lambda b, i: (b, i, 0)),
            pl.BlockSpec((1, tb, LANES), lambda b, i: (b, i, 0)),
        ],
        out_shape=[jax.ShapeDtypeStruct(qkv.shape, BF16),
                   jax.ShapeDtypeStruct((bsz, t, LANES), F32)],
        scratch_shapes=[pltpu.VMEM((tb + 2 * halo, w), F32)],
        name="dn_prep",
        compiler_params=_params(("parallel", "parallel")),
    )(qkv, qkv, qkv, ba, conv_w, alog_row, dtb_row)


def _split2(a):
    hi = a.astype(BF16)
    lo = (a - hi.astype(F32)).astype(BF16)
    return hi, lo


def _dn_setup(qkv_ref, bg_ref, *, d, reverse):
    c = DN_CHUNK
    bg = bg_ref[0]
    row = lax.broadcasted_iota(jnp.int32, (c, 2 * c), 0)
    lane2 = lax.broadcasted_iota(jnp.int32, (c, 2 * c), 1)
    p_part = lane2 < c
    col = jnp.where(p_part, lane2, lane2 - c)
    if reverse:
        causal, strict = row <= col, row < col
        last = 0
    else:
        causal, strict = row >= col, row > col
        last = c - 1
    eye_t = jnp.where(p_part, 0.0, jnp.where(row == col, 1.0, 0.0)).astype(F32)
    lane = lax.broadcasted_iota(jnp.int32, bg.shape, 1)
    g = jnp.where(lane >= 2 * DN_HEADS, bg, 0.0)
    g1 = g.astype(BF16)
    r1 = g - g1.astype(F32)
    g2 = r1.astype(BF16)
    g3 = (r1 - g2.astype(F32)).astype(BF16)
    tri = jnp.where(causal[:, :c], 1.0, 0.0).astype(BF16)
    gc3 = _dot(tri, jnp.concatenate([g1, g2, g3], axis=1))
    gc = gc3[:, :LANES] + gc3[:, LANES:2 * LANES] + gc3[:, 2 * LANES:]
    gct = gc.T
    gct2 = jnp.concatenate([gct, gct], axis=1)
    e_gc = jnp.exp(gc)
    g_last = gc[last:last + 1, :]
    e_rest = jnp.exp(g_last - gc)
    e_last = jnp.exp(g_last)
    g_lane = lambda hd: 2 * DN_HEADS + d * DN_HEADS + hd
    return types.SimpleNamespace(
        causal=causal, strict=strict, p_part=p_part, eye_t=eye_t, gc=gc, gct2=gct2, e_last=e_last,
        col_of=lambda a, hd: a[:, g_lane(hd):g_lane(hd) + 1],
        row_of=lambda a, hd: a[g_lane(hd):g_lane(hd) + 1, :],
        e_gc=e_gc, e_rest=e_rest,
        beta=lambda hd: bg[:, d * DN_HEADS + hd:d * DN_HEADS + hd + 1],
        load=lambda base, hd: qkv_ref[0, :, base + hd * DN_DK:base + (hd + 1) * DN_DK],
    )


def _dn_scan_kernel(qkv_f_ref, bg_f_ref, qkv_b_ref, bg_b_ref, of_ref, ob_ref,
                    s_ref, u0_ref, wq_ref, qk_ref, kd_ref, el_ref):
    c = DN_CHUNK

    @pl.when(pl.program_id(1) == 0)
    def _():
        for ref in (s_ref, u0_ref, wq_ref, qk_ref, kd_ref, el_ref):
            ref[...] = jnp.zeros_like(ref)

    out_refs = (of_ref, ob_ref)
    chains = [(d, hd) for hd in range(DN_HEADS) for d in (0, 1)]
    slot = lambda d, hd: d * DN_HEADS + hd

    carried = {}
    for d, hd in chains:
        j = slot(d, hd)
        s = s_ref[j]
        carried[j] = (s, _dot(wq_ref[j], s.astype(BF16)))
    for d, hd in chains:
        j = slot(d, hd)
        s, ws = carried[j]
        ub = (u0_ref[j] - ws[:c]).astype(BF16)
        out_refs[d][0, :, hd * DN_DV:(hd + 1) * DN_DV] = ws[c:] + _dot(qk_ref[j], ub)
        s_ref[j] = s * el_ref[j] + _dot_tn(kd_ref[j], ub)

    dirs = (_dn_setup(qkv_f_ref, bg_f_ref, d=0, reverse=False),
            _dn_setup(qkv_b_ref, bg_b_ref, d=1, reverse=True))
    m = {}
    for d, hd in chains:
        a = dirs[d]
        q, k = a.load(0, hd), a.load(DN_QK_W, hd)
        gram = _dot_nt(jnp.concatenate([q, k], axis=0), jnp.concatenate([k, k], axis=0))
        diff = a.col_of(a.gc, hd) - a.row_of(a.gct2, hd)
        decay = jnp.exp(jnp.where(a.causal, diff, -jnp.inf))
        qk_ref[slot(d, hd)] = (gram[:c] * decay)[:, :c].astype(BF16)
        m[d, hd] = jnp.where(a.strict, -a.beta(hd) * gram[c:] * decay, 0.0)

    zeros = jnp.zeros((c, 4 * c), BF16)
    levels = c.bit_length() - 1
    assert 1 << levels == c
    w = {}
    for d, hd in chains:
        a = dirs[d]
        w[d, hd] = jnp.where(a.p_part, m[d, hd], a.eye_t)
    for _ in range(levels):
        for d, hd in chains:
            wh, wl = _split2(w[d, hd])
            p = _dot(jnp.concatenate([wh, wl], axis=0),
                     jnp.concatenate([jnp.concatenate([wh, wl], axis=1), zeros], axis=0))
            r = p[:c, :2 * c] + p[:c, 2 * c:] + p[c:, :2 * c]
            w[d, hd] = jnp.where(dirs[d].p_part, r, w[d, hd] + r)

    for d, hd in chains:
        a = dirs[d]
        j = slot(d, hd)
        qf = a.load(0, hd).astype(F32)
        kf = a.load(DN_QK_W, hd).astype(F32)
        vf = a.load(2 * DN_QK_W, hd).astype(F32)
        eg = a.col_of(a.e_gc, hd)
        rhs = jnp.concatenate([vf * a.beta(hd), kf * (a.beta(hd) * eg)], axis=1).astype(BF16)
        sol = _dot(w[d, hd].astype(BF16),
                   jnp.concatenate([jnp.zeros_like(rhs), rhs], axis=0))
        u0_ref[j] = sol[:, :DN_DV]
        wq_ref[j] = jnp.concatenate([sol[:, DN_DV:], qf * eg], axis=0).astype(BF16)
        kd_ref[j] = (kf * a.col_of(a.e_rest, hd)).astype(BF16)
        el_ref[j] = jnp.broadcast_to(a.col_of(a.e_last, hd), (1, LANES))


def _dn_scan_call(qkv, bg):
    bsz, t, w = qkv.shape
    c = DN_CHUNK
    n = t // c
    nslot = 2 * DN_HEADS
    return pl.pallas_call(
        _dn_scan_kernel,
        grid=(bsz, n + 1),
        in_specs=[
            pl.BlockSpec((1, c, w), lambda b, i: (b, jnp.minimum(i, n - 1), 0)),
            pl.BlockSpec((1, c, LANES), lambda b, i: (b, jnp.minimum(i, n - 1), 0)),
            pl.BlockSpec((1, c, w), lambda b, i: (b, jnp.maximum(n - 1 - i, 0), 0)),
            pl.BlockSpec((1, c, LANES), lambda b, i: (b, jnp.maximum(n - 1 - i, 0), 0)),
        ],
        out_specs=[
            pl.BlockSpec((1, c, DN_V_W), lambda b, i: (b, jnp.maximum(i - 1, 0), 0)),
            pl.BlockSpec((1, c, DN_V_W), lambda b, i: (b, jnp.minimum(n - i, n - 1), 0)),
        ],
        out_shape=[jax.ShapeDtypeStruct((bsz, t, DN_V_W), F32)] * 2,
        scratch_shapes=[
            pltpu.VMEM((nslot, DN_DK, DN_DV), F32),
            pltpu.VMEM((nslot, c, DN_DV), F32),
            pltpu.VMEM((nslot, 2 * c, DN_DK), BF16),
            pltpu.VMEM((nslot, c, c), BF16),
            pltpu.VMEM((nslot, c, DN_DK), BF16),
            pltpu.VMEM((nslot, 1, LANES), F32),
        ],
        name="dn_scan",
        compiler_params=_params(("parallel", "arbitrary")),
    )(qkv, bg, qkv, bg)


def _attn_kernel(sink_ref, q_ref, kp_ref, kc_ref, kn_ref, vp_ref, vc_ref, vn_ref, o_ref, *, t):
    n = pl.program_id(1)
    wb = AT_WINDOW
    kw = jnp.concatenate([kp_ref[0], kc_ref[0], kn_ref[0]], axis=0)
    vw = jnp.concatenate([vp_ref[0], vc_ref[0], vn_ref[0]], axis=0)
    qi = lax.broadcasted_iota(jnp.int32, (wb, 3 * wb), 0)
    sj = lax.broadcasted_iota(jnp.int32, (wb, 3 * wb), 1)
    kpos = n * wb + sj - wb
    valid = (jnp.abs(sj - wb - qi) <= wb) & (kpos >= 0) & (kpos < t)
    bias = jnp.where(valid, 0.0, -jnp.inf).astype(F32)
    scale = AT_HEAD_DIM ** -0.5
    heads = [(kv, g) for kv in range(AT_KV_HEADS) for g in range(AT_GROUP)]
    head_slice = lambda kv, g: slice((kv * AT_GROUP + g) * AT_HEAD_DIM, (kv * AT_GROUP + g + 1) * AT_HEAD_DIM)
    kv_slice = lambda kv: slice(kv * AT_HEAD_DIM, (kv + 1) * AT_HEAD_DIM)
    s = [_dot_nt(q_ref[0, :, head_slice(kv, g)], kw[:, kv_slice(kv)]) * scale + bias for kv, g in heads]
    p, denom = [], []
    for i, (kv, g) in enumerate(heads):
        sink = sink_ref[kv * AT_GROUP + g]
        m = jnp.maximum(jnp.max(s[i], axis=-1, keepdims=True), sink)
        e = jnp.exp(s[i] - m)
        denom.append(jnp.sum(e, axis=-1, keepdims=True) + jnp.exp(sink - m))
        p.append(e.astype(BF16))
    for i, (kv, g) in enumerate(heads):
        o = _dot(p[i], vw[:, kv_slice(kv)]) / denom[i]
        o_ref[0, :, head_slice(kv, g)] = o.astype(BF16)


def _attn_call(q, k, v, sink):
    bsz, t, _ = q.shape
    wb = AT_WINDOW
    nb = t // wb
    cur = lambda b, i: (b, i, 0)
    prev = lambda b, i: (b, jnp.maximum(i - 1, 0), 0)
    nxt = lambda b, i: (b, jnp.minimum(i + 1, nb - 1), 0)
    kv_spec = lambda f: pl.BlockSpec((1, wb, AT_KV_W), f)
    return pl.pallas_call(
        functools.partial(_attn_kernel, t=t),
        grid=(bsz, nb),
        in_specs=[
            pl.BlockSpec(memory_space=pltpu.SMEM),
            pl.BlockSpec((1, wb, AT_Q_W), cur),
            kv_spec(prev), kv_spec(cur), kv_spec(nxt),
            kv_spec(prev), kv_spec(cur), kv_spec(nxt),
        ],
        out_specs=pl.BlockSpec((1, wb, AT_Q_W), cur),
        out_shape=jax.ShapeDtypeStruct((bsz, t, AT_Q_W), BF16),
        name="attn",
        compiler_params=_params(("parallel", "parallel")),
    )(sink, q, k, k, k, v, v, v)


def _mix_out_kernel(x_ref, mod_ref, of_ref, ob_ref, z_ref, at_ref, ga_ref, gb_ref, dnw_ref,
                    wa_ref, wb_ref, wo_ref, o_ref, dn_ref):
    x = x_ref[0]
    o = of_ref[0] + ob_ref[0]
    dnw = dnw_ref[...]
    for hd in range(DN_HEADS):
        sl = slice(hd * DN_DV, (hd + 1) * DN_DV)
        y = _rms(o[:, sl], dnw) * _silu(z_ref[0, :, sl].astype(F32))
        dn_ref[:, sl] = y.astype(BF16)
    ya = _dot(dn_ref[...], wa_ref[...])
    yb = _dot(at_ref[0], wb_ref[...])
    merged = _sigmoid(ga_ref[0].astype(F32)) * ya + _sigmoid(gb_ref[0].astype(F32)) * yb
    gate = mod_ref[0, 5:6, :]
    o_ref[0] = x + gate * _dot(merged.astype(BF16), wo_ref[...])


def _mix_out_call(x, mod, o_f, o_b, z, at, ga, gb, dn_norm, wa, wb, wo, *, tm):
    bsz, t, d = x.shape
    tok = lambda w: pl.BlockSpec((1, tm, w), lambda b, i: (b, i, 0))
    return pl.pallas_call(
        _mix_out_kernel,
        grid=(bsz, t // tm),
        in_specs=[
            tok(d),
            pl.BlockSpec((1, N_MOD, d), lambda b, i: (b, 0, 0)),
            tok(DN_V_W), tok(DN_V_W), tok(DN_V_W), tok(AT_Q_W), tok(d), tok(d),
            _resident((1, DN_DV)),
            _resident(wa.shape), _resident(wb.shape), _resident(wo.shape),
        ],
        out_specs=tok(d),
        out_shape=jax.ShapeDtypeStruct(x.shape, F32),
        scratch_shapes=[pltpu.VMEM((tm, DN_V_W), BF16)],
        name="mix_out",
        compiler_params=_params(("parallel", "parallel")),
    )(x, mod, o_f, o_b, z, at, ga, gb, dn_norm, wa, wb, wo)


def _rope_tables(t):
    half = ROPE_DIM // 2
    inv = jnp.power(jnp.float32(ROPE_THETA), -jnp.arange(half, dtype=F32) / half)
    ang = jnp.arange(t, dtype=F32)[:, None] * inv[None, :]
    cos, sin = jnp.cos(ang), jnp.sin(ang)
    rest = LANES - ROPE_DIM
    cos_t = jnp.concatenate([cos, cos, jnp.ones((t, rest), F32)], axis=1)
    zeros = jnp.zeros((t, half), F32)
    sin_lo = jnp.concatenate([-sin, zeros, jnp.zeros((t, rest), F32)], axis=1)
    sin_hi = jnp.concatenate([zeros, sin, jnp.zeros((t, rest), F32)], axis=1)
    return cos_t, sin_lo, sin_hi


def _prepare_weights(w_ada, b_ada, ffn1_norm, ffn1_w_in, ffn1_w_out, mix_norm, w_in, conv_w, a_log,
                     dt_bias, dn_norm, attn_sink, w_proj_a, w_proj_b, w_out, ffn2_norm, ffn2_w_in,
                     ffn2_w_out, final_norm):
    row = lambda a: a.reshape(1, -1).astype(F32)
    offs = [0]
    for wdt in (DN_QKV_W, DN_V_W, 2 * DN_HEADS, 2 * DN_HEADS, AT_Q_W, AT_KV_W, AT_KV_W, D_MODEL, D_MODEL):
        offs.append(offs[-1] + wdt)
    seg = lambda i, j: w_in[:, offs[i]:offs[j]]
    w_ba = jnp.pad(seg(2, 4), ((0, 0), (0, LANES - 4 * DN_HEADS)))
    mix_ws = [seg(0, 1), seg(1, 2), w_ba, seg(4, 5), seg(5, 6), seg(6, 7), seg(7, 8), seg(8, 9)]
    lane_pad = lambda a: jnp.pad(a.reshape(1, -1).astype(F32),
                                 ((0, 0), (2 * DN_HEADS, LANES - 4 * DN_HEADS)))
    return dict(
        w_ada=w_ada.astype(BF16), b_ada=row(b_ada),
        ffn1=(row(ffn1_norm), ffn1_w_in[:, :D_FF].astype(BF16), ffn1_w_in[:, D_FF:].astype(BF16),
              ffn1_w_out.astype(BF16)),
        ffn2=(row(ffn2_norm), ffn2_w_in[:, :D_FF].astype(BF16), ffn2_w_in[:, D_FF:].astype(BF16),
              ffn2_w_out.astype(BF16)),
        mix_norm=row(mix_norm), mix_ws=[w.astype(BF16) for w in mix_ws],
        conv_w=conv_w.astype(F32), a_log=lane_pad(a_log), dt_bias=lane_pad(dt_bias),
        dn_norm=row(dn_norm), sink=attn_sink.astype(F32),
        w_proj_a=w_proj_a.astype(BF16), w_proj_b=w_proj_b.astype(BF16), w_out=w_out.astype(BF16),
        final_norm=row(final_norm),
    )


def _tile(t, want):
    tile = min(t, want)
    assert t % tile == 0, (t, tile)
    return tile


def _trunk(x, c, p):
    bsz, t, d = x.shape
    tm = _tile(t, 512)
    mod = _mod_call(c, p["w_ada"], p["b_ada"]).reshape(bsz, N_MOD, d)
    x = _ffn_call(x, mod, *p["ffn1"], p["final_norm"], sub=0, final=False, tm=tm)
    qkv, z, ba, q, k, v, ga, gb = _mix_in_call(x, mod, p["mix_norm"], _rope_tables(t), p["mix_ws"], tm=tm)
    qkv_n, bg = _dn_prep_call(qkv, ba, p["conv_w"], p["a_log"], p["dt_bias"], tb=_tile(t, 256))
    o_f, o_b = _dn_scan_call(qkv_n, bg)
    at = _attn_call(q, k, v, p["sink"])
    x = _mix_out_call(x, mod, o_f, o_b, z, at, ga, gb, p["dn_norm"], p["w_proj_a"], p["w_proj_b"],
                      p["w_out"], tm=tm)
    return _ffn_call(x, mod, *p["ffn2"], p["final_norm"], sub=2, final=True, tm=tm)


def kernel(x_prompt, x_sample, c_prompt, c_sample, w_ada, b_ada, ffn1_norm, ffn1_w_in, ffn1_w_out, mix_norm, w_in, conv_w, a_log, dt_bias, dn_norm, attn_sink, w_proj_a, w_proj_b, w_out, ffn2_norm, ffn2_w_in, ffn2_w_out, final_norm):
    depth = w_ada.shape[0]
    layers = [
        _prepare_weights(w_ada[l], b_ada[l], ffn1_norm[l], ffn1_w_in[l], ffn1_w_out[l], mix_norm[l],
                         w_in[l], conv_w[l], a_log[l], dt_bias[l], dn_norm[l], attn_sink[l],
                         w_proj_a[l], w_proj_b[l], w_out[l], ffn2_norm[l], ffn2_w_in[l],
                         ffn2_w_out[l], final_norm)
        for l in range(depth)
    ]
    assert depth == 1, "final RMSNorm is fused into the last layer's second FFN"
    y_prompt = _trunk(x_prompt, c_prompt, layers[0])
    y_sample = _trunk(x_sample, c_sample, layers[0])
    return (y_prompt, y_sample)
```

```python
import functools
import types

import jax
import jax.numpy as jnp
from jax import lax
from jax.experimental import pallas as pl
from jax.experimental.pallas import tpu as pltpu

D_MODEL = 1024
DN_HEADS = 8
DN_DK = 128
DN_DV = 128
DN_CONV = 5
DN_CHUNK = 64
DN_SUB = 2
AT_HEADS = 8
AT_KV_HEADS = 2
AT_HEAD_DIM = 128
AT_WINDOW = 128
AT_QB = 2
ROPE_THETA = 500000.0
ROPE_DIM = AT_HEAD_DIM // 4
D_FF = 2816
EPS = 1e-6
N_MOD = 9

DN_QK_W = DN_HEADS * DN_DK
DN_V_W = DN_HEADS * DN_DV
DN_QKV_W = 2 * DN_QK_W + DN_V_W
AT_Q_W = AT_HEADS * AT_HEAD_DIM
AT_KV_W = AT_KV_HEADS * AT_HEAD_DIM
AT_GROUP = AT_HEADS // AT_KV_HEADS

LANES = 128
BF16_SUBLANES = 16
VMEM_LIMIT = 56 * 1024 * 1024

F32 = jnp.float32
BF16 = jnp.bfloat16


def _dot(a, b):
    return jnp.dot(a, b, preferred_element_type=F32)


def _dot_nt(a, b):
    return lax.dot_general(a, b, (((1,), (1,)), ((), ())), preferred_element_type=F32)


def _dot_tn(a, b):
    return lax.dot_general(a, b, (((0,), (0,)), ((), ())), preferred_element_type=F32)


def _rms(x, w):
    return x * lax.rsqrt(jnp.mean(x * x, axis=-1, keepdims=True) + EPS) * w


def _ada_rms(x, w, shift, scale):
    return _rms(x, w) * (1.0 + scale) + shift


def _sigmoid(x):
    return 1.0 / (1.0 + jnp.exp(-x))


def _silu(x):
    return x * _sigmoid(x)


def _resident(shape):
    nd = len(shape)
    return pl.BlockSpec(shape, lambda *_: (0,) * nd, pipeline_mode=pl.Buffered(1))


def _params(sem):
    return pltpu.CompilerParams(dimension_semantics=sem, vmem_limit_bytes=VMEM_LIMIT)


def _mod_kernel(c_ref, w_ref, b_ref, o_ref):
    c = c_ref[...]
    o_ref[...] = _dot(_silu(c).astype(BF16), w_ref[...]) + b_ref[...]


def _mod_call(c, w_ada, b_ada):
    bsz = c.shape[0]
    n = w_ada.shape[1]
    bn = n // N_MOD
    return pl.pallas_call(
        _mod_kernel,
        grid=(N_MOD,),
        in_specs=[
            pl.BlockSpec((bsz, D_MODEL), lambda j: (0, 0)),
            pl.BlockSpec((D_MODEL, bn), lambda j: (0, j)),
            pl.BlockSpec((1, bn), lambda j: (0, j)),
        ],
        out_specs=pl.BlockSpec((bsz, bn), lambda j: (0, j)),
        out_shape=jax.ShapeDtypeStruct((bsz, n), F32),
        name="mod",
        compiler_params=_params(("arbitrary",)),
    )(c, w_ada, b_ada)


def _ffn_kernel(x_ref, mod_ref, nw_ref, wg_ref, wu_ref, wo_ref, fw_ref, o_ref, *, sub, final):
    x = x_ref[0]
    shift = mod_ref[0, 3 * sub:3 * sub + 1, :]
    scale = mod_ref[0, 3 * sub + 1:3 * sub + 2, :]
    gate = mod_ref[0, 3 * sub + 2:3 * sub + 3, :]
    h = _ada_rms(x, nw_ref[...], shift, scale).astype(BF16)
    g = _dot(h, wg_ref[...])
    u = _dot(h, wu_ref[...])
    a = (_silu(g) * u).astype(BF16)
    y = x + 0.5 * gate * _dot(a, wo_ref[...])
    if final:
        y = _rms(y, fw_ref[...])
    o_ref[0] = y


def _ffn_call(x, mod, norm_w, wg, wu, wo, final_w, *, sub, final, tm):
    bsz, t, d = x.shape
    return pl.pallas_call(
        functools.partial(_ffn_kernel, sub=sub, final=final),
        grid=(bsz, t // tm),
        in_specs=[
            pl.BlockSpec((1, tm, d), lambda b, i: (b, i, 0)),
            pl.BlockSpec((1, N_MOD, d), lambda b, i: (b, 0, 0)),
            _resident((1, d)),
            _resident(wg.shape),
            _resident(wu.shape),
            _resident(wo.shape),
            _resident((1, d)),
        ],
        out_specs=pl.BlockSpec((1, tm, d), lambda b, i: (b, i, 0)),
        out_shape=jax.ShapeDtypeStruct(x.shape, F32),
        name="ffn",
        compiler_params=_params(("parallel", "parallel")),
    )(x, mod, norm_w, wg, wu, wo, final_w)


def _rope(x, cos, sin_lo, sin_hi):
    half = ROPE_DIM // 2
    up = pltpu.roll(x, LANES - half, 1)
    down = pltpu.roll(x, half, 1)
    return x * cos + up * sin_lo + down * sin_hi


def _mix_in_kernel(x_ref, mod_ref, nw_ref, cos_ref, slo_ref, shi_ref,
                   w_qkv_ref, w_z_ref, w_ba_ref, w_q_ref, w_k_ref, w_v_ref, w_ga_ref, w_gb_ref,
                   qkv_ref, z_ref, ba_ref, q_ref, k_ref, v_ref, ga_ref, gb_ref):
    x = x_ref[0]
    h = _ada_rms(x, nw_ref[...], mod_ref[0, 3:4, :], mod_ref[0, 4:5, :]).astype(BF16)
    qkv_ref[0] = _dot(h, w_qkv_ref[...]).astype(BF16)
    z_ref[0] = _dot(h, w_z_ref[...]).astype(BF16)
    ba_ref[0] = _dot(h, w_ba_ref[...])
    v_ref[0] = _dot(h, w_v_ref[...]).astype(BF16)
    ga_ref[0] = _dot(h, w_ga_ref[...]).astype(BF16)
    gb_ref[0] = _dot(h, w_gb_ref[...]).astype(BF16)
    cos, slo, shi = cos_ref[...], slo_ref[...], shi_ref[...]
    q = _dot(h, w_q_ref[...])
    for hd in range(AT_HEADS):
        sl = slice(hd * LANES, (hd + 1) * LANES)
        q_ref[0, :, sl] = _rope(q[:, sl], cos, slo, shi).astype(BF16)
    k = _dot(h, w_k_ref[...])
    for hd in range(AT_KV_HEADS):
        sl = slice(hd * LANES, (hd + 1) * LANES)
        k_ref[0, :, sl] = _rope(k[:, sl], cos, slo, shi).astype(BF16)


def _mix_in_call(x, mod, norm_w, tabs, ws, *, tm):
    bsz, t, d = x.shape
    tok = lambda w, dt: (pl.BlockSpec((1, tm, w), lambda b, i: (b, i, 0)),
                         jax.ShapeDtypeStruct((bsz, t, w), dt))
    outs = [tok(DN_QKV_W, BF16), tok(DN_V_W, BF16), tok(LANES, F32), tok(AT_Q_W, BF16),
            tok(AT_KV_W, BF16), tok(AT_KV_W, BF16), tok(D_MODEL, BF16), tok(D_MODEL, BF16)]
    tab_spec = pl.BlockSpec((tm, LANES), lambda b, i: (i, 0))
    return pl.pallas_call(
        _mix_in_kernel,
        grid=(bsz, t // tm),
        in_specs=[
            pl.BlockSpec((1, tm, d), lambda b, i: (b, i, 0)),
            pl.BlockSpec((1, N_MOD, d), lambda b, i: (b, 0, 0)),
            _resident((1, d)),
            tab_spec, tab_spec, tab_spec,
        ] + [_resident(w.shape) for w in ws],
        out_specs=[o[0] for o in outs],
        out_shape=[o[1] for o in outs],
        name="mix_in",
        compiler_params=_params(("parallel", "parallel")),
    )(x, mod, norm_w, *tabs, *ws)


def _dn_prep_kernel(cur_ref, prev_ref, next_ref, ba_ref, cw_ref, alog_ref, dtb_ref,
                    qkv_ref, bg_ref, pad_ref, *, tb):
    i = pl.program_id(1)
    last = pl.num_programs(1) - 1
    halo = BF16_SUBLANES
    prev = jnp.where(i == 0, 0.0, prev_ref[0].astype(F32))
    nxt = jnp.where(i == last, 0.0, next_ref[0].astype(F32))
    pad_ref[0:halo, :] = prev
    pad_ref[halo:halo + tb, :] = cur_ref[0].astype(F32)
    pad_ref[halo + tb:, :] = nxt
    acc = None
    for j in range(DN_CONV):
        off = halo - DN_CONV // 2 + j
        term = pad_ref[off:off + tb, :] * cw_ref[j:j + 1, :]
        acc = term if acc is None else acc + term
    y = _silu(acc)
    for hd in range(2 * DN_HEADS):
        sl = slice(hd * LANES, (hd + 1) * LANES)
        yh = y[:, sl]
        n = yh * lax.rsqrt(jnp.sum(yh * yh, axis=-1, keepdims=True) + EPS)
        if hd < DN_HEADS:
            n = n * (DN_DK ** -0.5)
        qkv_ref[0, :, sl] = n.astype(BF16)
    qkv_ref[0, :, 2 * DN_QK_W:] = y[:, 2 * DN_QK_W:].astype(BF16)
    ba = ba_ref[0]
    lane = lax.broadcasted_iota(jnp.int32, ba.shape, 1)
    beta = _sigmoid(ba)
    z = ba + dtb_ref[...]
    softplus = jnp.maximum(z, 0.0) + jnp.log(1.0 + jnp.exp(-jnp.abs(z)))
    g = -jnp.exp(alog_ref[...]) * softplus
    nbg = 2 * DN_HEADS
    bg_ref[0] = jnp.where(lane < nbg, beta, jnp.where(lane < 2 * nbg, g, 0.0))


def _dn_prep_call(qkv, ba, conv_w, alog_row, dtb_row, *, tb):
    bsz, t, w = qkv.shape
    halo = BF16_SUBLANES
    r = tb // halo
    nh = t // halo
    return pl.pallas_call(
        functools.partial(_dn_prep_kernel, tb=tb),
        grid=(bsz, t // tb),
        in_specs=[
            pl.BlockSpec((1, tb, w), lambda b, i: (b, i, 0)),
            pl.BlockSpec((1, halo, w), lambda b, i: (b, jnp.maximum(i * r - 1, 0), 0)),
            pl.BlockSpec((1, halo, w), lambda b, i: (b, jnp.minimum((i + 1) * r, nh - 1), 0)),
            pl.BlockSpec((1, tb, LANES), lambda b, i: (b, i, 0)),
            _resident(conv_w.shape),
            _resident((1, LANES)),
            _resident((1, LANES)),
        ],
        out_specs=[
            pl.BlockSpec((1, tb, w), lambda b, i: (b, i, 0)),
            pl.BlockSpec((1, tb, LANES), lambda b, i: (b, i, 0)),
        ],
        out_shape=[jax.ShapeDtypeStruct(qkv.shape, BF16),
                   jax.ShapeDtypeStruct((bsz, t, LANES), F32)],
        scratch_shapes=[pltpu.VMEM((tb + 2 * halo, w), F32)],
        name="dn_prep",
        compiler_params=_params(("parallel", "parallel")),
    )(qkv, qkv, qkv, ba, conv_w, alog_row, dtb_row)


def _split2(a):
    hi = a.astype(BF16)
    lo = (a - hi.astype(F32)).astype(BF16)
    return hi, lo


def _dn_setup(qkv_ref, bg_ref, r0, *, d, reverse):
    c = DN_CHUNK
    bg = bg_ref[0, r0:r0 + c, :]
    row = lax.broadcasted_iota(jnp.int32, (c, 2 * c), 0)
    lane2 = lax.broadcasted_iota(jnp.int32, (c, 2 * c), 1)
    p_part = lane2 < c
    col = jnp.where(p_part, lane2, lane2 - c)
    if reverse:
        causal, strict = row <= col, row < col
        last = 0
    else:
        causal, strict = row >= col, row > col
        last = c - 1
    eye_t = jnp.where(p_part, 0.0, jnp.where(row == col, 1.0, 0.0)).astype(F32)
    lane = lax.broadcasted_iota(jnp.int32, bg.shape, 1)
    g = jnp.where(lane >= 2 * DN_HEADS, bg, 0.0)
    g1 = g.astype(BF16)
    r1 = g - g1.astype(F32)
    g2 = r1.astype(BF16)
    g3 = (r1 - g2.astype(F32)).astype(BF16)
    tri = jnp.where(causal[:, :c], 1.0, 0.0).astype(BF16)
    gc3 = _dot(tri, jnp.concatenate([g1, g2, g3], axis=1))
    gc = gc3[:, :LANES] + gc3[:, LANES:2 * LANES] + gc3[:, 2 * LANES:]
    gct = gc.T
    gct2 = jnp.concatenate([gct, gct], axis=1)
    e_gc = jnp.exp(gc)
    g_last = gc[last:last + 1, :]
    e_rest = jnp.exp(g_last - gc)
    e_last = jnp.exp(g_last)
    g_lane = lambda hd: 2 * DN_HEADS + d * DN_HEADS + hd
    return types.SimpleNamespace(
        causal=causal, strict=strict, p_part=p_part, eye_t=eye_t, gc=gc, gct2=gct2, e_last=e_last,
        col_of=lambda a, hd: a[:, g_lane(hd):g_lane(hd) + 1],
        row_of=lambda a, hd: a[g_lane(hd):g_lane(hd) + 1, :],
        e_gc=e_gc, e_rest=e_rest,
        beta=lambda hd: bg[:, d * DN_HEADS + hd:d * DN_HEADS + hd + 1],
        load=lambda base, hd: qkv_ref[0, r0:r0 + c, base + hd * DN_DK:base + (hd + 1) * DN_DK],
    )


def _dn_scan_kernel(qkv_f_ref, bg_f_ref, qkv_b_ref, bg_b_ref, of_ref, ob_ref,
                    s_ref, u0_ref, wq_ref, qk_ref, kd_ref, el_ref):
    c = DN_CHUNK

    @pl.when(pl.program_id(1) == 0)
    def _():
        for ref in (s_ref, u0_ref, wq_ref, qk_ref, kd_ref, el_ref):
            ref[...] = jnp.zeros_like(ref)

    out_refs = (of_ref, ob_ref)
    row0 = lambda sub, d: (DN_SUB - 1 - sub if d else sub) * c
    slot = lambda sub, d, hd: (sub * 2 + d) * DN_HEADS + hd
    pairs = [(d, hd) for hd in range(DN_HEADS) for d in (0, 1)]
    chains = [(sub, d, hd) for sub in range(DN_SUB) for d, hd in pairs]

    for sub in range(DN_SUB):
        carried = {}
        for d, hd in pairs:
            s = s_ref[d * DN_HEADS + hd]
            carried[d, hd] = (s, _dot(wq_ref[slot(sub, d, hd)], s.astype(BF16)))
        for d, hd in pairs:
            j = slot(sub, d, hd)
            s, ws = carried[d, hd]
            ub = (u0_ref[j] - ws[:c]).astype(BF16)
            r0 = row0(sub, d)
            out_refs[d][0, r0:r0 + c, hd * DN_DV:(hd + 1) * DN_DV] = ws[c:] + _dot(qk_ref[j], ub)
            s_ref[d * DN_HEADS + hd] = s * el_ref[j] + _dot_tn(kd_ref[j], ub)

    in_refs = ((qkv_f_ref, bg_f_ref), (qkv_b_ref, bg_b_ref))
    setup = {(sub, d): _dn_setup(*in_refs[d], row0(sub, d), d=d, reverse=bool(d))
             for sub in range(DN_SUB) for d in (0, 1)}
    m = {}
    for sub, d, hd in chains:
        a = setup[sub, d]
        q, k = a.load(0, hd), a.load(DN_QK_W, hd)
        gram = _dot_nt(jnp.concatenate([q, k], axis=0), jnp.concatenate([k, k], axis=0))
        diff = a.col_of(a.gc, hd) - a.row_of(a.gct2, hd)
        decay = jnp.exp(jnp.where(a.causal, diff, -jnp.inf))
        qk_ref[slot(sub, d, hd)] = (gram[:c] * decay)[:, :c].astype(BF16)
        m[sub, d, hd] = jnp.where(a.strict, -a.beta(hd) * gram[c:] * decay, 0.0)

    levels = c.bit_length() - 1
    assert 1 << levels == c
    w = {}
    for sub, d, hd in chains:
        a = setup[sub, d]
        w[sub, d, hd] = jnp.where(a.p_part, m[sub, d, hd], a.eye_t)
    for _ in range(levels):
        for key in chains:
            p_part = setup[key[0], key[1]].p_part
            wh, wl = _split2(w[key])
            p = _dot(jnp.concatenate([wh[:, :c], wl[:, :c]], axis=0),
                     jnp.concatenate([wh, wl], axis=1))
            r = p[:c, :2 * c] + p[:c, 2 * c:] + p[c:, :2 * c]
            w[key] = jnp.where(p_part, r, w[key] + r)

    for sub, d, hd in chains:
        a = setup[sub, d]
        j = slot(sub, d, hd)
        qf = a.load(0, hd).astype(F32)
        kf = a.load(DN_QK_W, hd).astype(F32)
        vf = a.load(2 * DN_QK_W, hd).astype(F32)
        eg = a.col_of(a.e_gc, hd)
        rhs = jnp.concatenate([vf * a.beta(hd), kf * (a.beta(hd) * eg)], axis=1).astype(BF16)
        sol = _dot(w[sub, d, hd].astype(BF16),
                   jnp.concatenate([jnp.zeros_like(rhs), rhs], axis=0))
        u0_ref[j] = sol[:, :DN_DV]
        wq_ref[j] = jnp.concatenate([sol[:, DN_DV:], qf * eg], axis=0).astype(BF16)
        kd_ref[j] = (kf * a.col_of(a.e_rest, hd)).astype(BF16)
        el_ref[j] = jnp.broadcast_to(a.col_of(a.e_last, hd), (1, LANES))


def _dn_scan_call(qkv, bg):
    bsz, t, w = qkv.shape
    c = DN_CHUNK
    rows = DN_SUB * c
    assert t % rows == 0, (t, rows)
    n = t // rows
    nslot = DN_SUB * 2 * DN_HEADS
    return pl.pallas_call(
        _dn_scan_kernel,
        grid=(bsz, n + 1),
        in_specs=[
            pl.BlockSpec((1, rows, w), lambda b, i: (b, jnp.minimum(i, n - 1), 0)),
            pl.BlockSpec((1, rows, LANES), lambda b, i: (b, jnp.minimum(i, n - 1), 0)),
            pl.BlockSpec((1, rows, w), lambda b, i: (b, jnp.maximum(n - 1 - i, 0), 0)),
            pl.BlockSpec((1, rows, LANES), lambda b, i: (b, jnp.maximum(n - 1 - i, 0), 0)),
        ],
        out_specs=[
            pl.BlockSpec((1, rows, DN_V_W), lambda b, i: (b, jnp.maximum(i - 1, 0), 0)),
            pl.BlockSpec((1, rows, DN_V_W), lambda b, i: (b, jnp.minimum(n - i, n - 1), 0)),
        ],
        out_shape=[jax.ShapeDtypeStruct((bsz, t, DN_V_W), F32)] * 2,
        scratch_shapes=[
            pltpu.VMEM((2 * DN_HEADS, DN_DK, DN_DV), F32),
            pltpu.VMEM((nslot, c, DN_DV), F32),
            pltpu.VMEM((nslot, 2 * c, DN_DK), BF16),
            pltpu.VMEM((nslot, c, c), BF16),
            pltpu.VMEM((nslot, c, DN_DK), BF16),
            pltpu.VMEM((nslot, 1, LANES), F32),
        ],
        name="dn_scan",
        compiler_params=_params(("parallel", "arbitrary")),
    )(qkv, bg, qkv, bg)


def _attn_kernel(sink_ref, q_ref, kp_ref, kc_ref, kn_ref, vp_ref, vc_ref, vn_ref, o_ref, *, t):
    n = pl.program_id(1)
    wb = AT_WINDOW
    kw = jnp.concatenate([kp_ref[0], kc_ref[0], kn_ref[0]], axis=0)
    vw = jnp.concatenate([vp_ref[0], vc_ref[0], vn_ref[0]], axis=0)
    qi = lax.broadcasted_iota(jnp.int32, (wb, 3 * wb), 0)
    sj = lax.broadcasted_iota(jnp.int32, (wb, 3 * wb), 1)
    inband = jnp.abs(sj - wb - qi) <= wb
    scale = AT_HEAD_DIM ** -0.5
    head_slice = lambda kv, g: slice((kv * AT_GROUP + g) * AT_HEAD_DIM, (kv * AT_GROUP + g + 1) * AT_HEAD_DIM)
    kv_slice = lambda kv: slice(kv * AT_HEAD_DIM, (kv + 1) * AT_HEAD_DIM)
    work = [(j, kv, g) for j in range(AT_QB) for kv in range(AT_KV_HEADS) for g in range(AT_GROUP)]
    bias = []
    for j in range(AT_QB):
        kpos = (n * AT_QB + j) * wb + sj - wb
        bias.append(jnp.where(inband & (kpos >= 0) & (kpos < t), 0.0, -jnp.inf).astype(F32))
    s = [_dot_nt(q_ref[0, j * wb:(j + 1) * wb, head_slice(kv, g)], kw[j * wb:(j + 3) * wb, kv_slice(kv)])
         * scale + bias[j] for j, kv, g in work]
    p, denom = [], []
    for i, (j, kv, g) in enumerate(work):
        sink = sink_ref[kv * AT_GROUP + g]
        m = jnp.maximum(jnp.max(s[i], axis=-1, keepdims=True), sink)
        e = jnp.exp(s[i] - m)
        denom.append(jnp.sum(e, axis=-1, keepdims=True) + jnp.exp(sink - m))
        p.append(e.astype(BF16))
    for i, (j, kv, g) in enumerate(work):
        o = _dot(p[i], vw[j * wb:(j + 3) * wb, kv_slice(kv)]) / denom[i]
        o_ref[0, j * wb:(j + 1) * wb, head_slice(kv, g)] = o.astype(BF16)


def _attn_call(q, k, v, sink):
    bsz, t, _ = q.shape
    wb = AT_WINDOW
    nb = t // wb
    rows = AT_QB * wb
    assert t % rows == 0, (t, rows)
    cur = lambda b, i: (b, i, 0)
    prev = lambda b, i: (b, jnp.maximum(i * AT_QB - 1, 0), 0)
    nxt = lambda b, i: (b, jnp.minimum((i + 1) * AT_QB, nb - 1), 0)
    halo_spec = lambda f: pl.BlockSpec((1, wb, AT_KV_W), f)
    cur_spec = pl.BlockSpec((1, rows, AT_KV_W), cur)
    return pl.pallas_call(
        functools.partial(_attn_kernel, t=t),
        grid=(bsz, t // rows),
        in_specs=[
            pl.BlockSpec(memory_space=pltpu.SMEM),
            pl.BlockSpec((1, rows, AT_Q_W), cur),
            halo_spec(prev), cur_spec, halo_spec(nxt),
            halo_spec(prev), cur_spec, halo_spec(nxt),
        ],
        out_specs=pl.BlockSpec((1, rows, AT_Q_W), cur),
        out_shape=jax.ShapeDtypeStruct((bsz, t, AT_Q_W), BF16),
        name="attn",
        compiler_params=_params(("parallel", "parallel")),
    )(sink, q, k, k, k, v, v, v)


def _mix_out_kernel(x_ref, mod_ref, of_ref, ob_ref, z_ref, at_ref, ga_ref, gb_ref, dnw_ref,
                    wa_ref, wb_ref, wo_ref, o_ref, dn_ref):
    x = x_ref[0]
    o = of_ref[0] + ob_ref[0]
    dnw = dnw_ref[...]
    for hd in range(DN_HEADS):
        sl = slice(hd * DN_DV, (hd + 1) * DN_DV)
        y = _rms(o[:, sl], dnw) * _silu(z_ref[0, :, sl].astype(F32))
        dn_ref[:, sl] = y.astype(BF16)
    ya = _dot(dn_ref[...], wa_ref[...])
    yb = _dot(at_ref[0], wb_ref[...])
    merged = _sigmoid(ga_ref[0].astype(F32)) * ya + _sigmoid(gb_ref[0].astype(F32)) * yb
    gate = mod_ref[0, 5:6, :]
    o_ref[0] = x + gate * _dot(merged.astype(BF16), wo_ref[...])


def _mix_out_call(x, mod, o_f, o_b, z, at, ga, gb, dn_norm, wa, wb, wo, *, tm):
    bsz, t, d = x.shape
    tok = lambda w: pl.BlockSpec((1, tm, w), lambda b, i: (b, i, 0))
    return pl.pallas_call(
        _mix_out_kernel,
        grid=(bsz, t // tm),
        in_specs=[
            tok(d),
            pl.BlockSpec((1, N_MOD, d), lambda b, i: (b, 0, 0)),
            tok(DN_V_W), tok(DN_V_W), tok(DN_V_W), tok(AT_Q_W), tok(d), tok(d),
            _resident((1, DN_DV)),
            _resident(wa.shape), _resident(wb.shape), _resident(wo.shape),
        ],
        out_specs=tok(d),
        out_shape=jax.ShapeDtypeStruct(x.shape, F32),
        scratch_shapes=[pltpu.VMEM((tm, DN_V_W), BF16)],
        name="mix_out",
        compiler_params=_params(("parallel", "parallel")),
    )(x, mod, o_f, o_b, z, at, ga, gb, dn_norm, wa, wb, wo)


def _rope_tables(t):
    half = ROPE_DIM // 2
    inv = jnp.power(jnp.float32(ROPE_THETA), -jnp.arange(half, dtype=F32) / half)
    ang = jnp.arange(t, dtype=F32)[:, None] * inv[None, :]
    cos, sin = jnp.cos(ang), jnp.sin(ang)
    rest = LANES - ROPE_DIM
    cos_t = jnp.concatenate([cos, cos, jnp.ones((t, rest), F32)], axis=1)
    zeros = jnp.zeros((t, half), F32)
    sin_lo = jnp.concatenate([-sin, zeros, jnp.zeros((t, rest), F32)], axis=1)
    sin_hi = jnp.concatenate([zeros, sin, jnp.zeros((t, rest), F32)], axis=1)
    return cos_t, sin_lo, sin_hi


def _prepare_weights(w_ada, b_ada, ffn1_norm, ffn1_w_in, ffn1_w_out, mix_norm, w_in, conv_w, a_log,
                     dt_bias, dn_norm, attn_sink, w_proj_a, w_proj_b, w_out, ffn2_norm, ffn2_w_in,
                     ffn2_w_out, final_norm):
    row = lambda a: a.reshape(1, -1).astype(F32)
    offs = [0]
    for wdt in (DN_QKV_W, DN_V_W, 2 * DN_HEADS, 2 * DN_HEADS, AT_Q_W, AT_KV_W, AT_KV_W, D_MODEL, D_MODEL):
        offs.append(offs[-1] + wdt)
    seg = lambda i, j: w_in[:, offs[i]:offs[j]]
    w_ba = jnp.pad(seg(2, 4), ((0, 0), (0, LANES - 4 * DN_HEADS)))
    mix_ws = [seg(0, 1), seg(1, 2), w_ba, seg(4, 5), seg(5, 6), seg(6, 7), seg(7, 8), seg(8, 9)]
    lane_pad = lambda a: jnp.pad(a.reshape(1, -1).astype(F32),
                                 ((0, 0), (2 * DN_HEADS, LANES - 4 * DN_HEADS)))
    return dict(
        w_ada=w_ada.astype(BF16), b_ada=row(b_ada),
        ffn1=(row(ffn1_norm), ffn1_w_in[:, :D_FF].astype(BF16), ffn1_w_in[:, D_FF:].astype(BF16),
              ffn1_w_out.astype(BF16)),
        ffn2=(row(ffn2_norm), ffn2_w_in[:, :D_FF].astype(BF16), ffn2_w_in[:, D_FF:].astype(BF16),
              ffn2_w_out.astype(BF16)),
        mix_norm=row(mix_norm), mix_ws=[w.astype(BF16) for w in mix_ws],
        conv_w=conv_w.astype(F32), a_log=lane_pad(a_log), dt_bias=lane_pad(dt_bias),
        dn_norm=row(dn_norm), sink=attn_sink.astype(F32),
        w_proj_a=w_proj_a.astype(BF16), w_proj_b=w_proj_b.astype(BF16), w_out=w_out.astype(BF16),
        final_norm=row(final_norm),
    )


def _tile(t, want):
    tile = min(t, want)
    assert t % tile == 0, (t, tile)
    return tile


def _trunk(x, c, p):
    bsz, t, d = x.shape
    tm = _tile(t, 512)
    mod = _mod_call(c, p["w_ada"], p["b_ada"]).reshape(bsz, N_MOD, d)
    x = _ffn_call(x, mod, *p["ffn1"], p["final_norm"], sub=0, final=False, tm=tm)
    qkv, z, ba, q, k, v, ga, gb = _mix_in_call(x, mod, p["mix_norm"], _rope_tables(t), p["mix_ws"], tm=tm)
    qkv_n, bg = _dn_prep_call(qkv, ba, p["conv_w"], p["a_log"], p["dt_bias"], tb=_tile(t, 512))
    o_f, o_b = _dn_scan_call(qkv_n, bg)
    at = _attn_call(q, k, v, p["sink"])
    x = _mix_out_call(x, mod, o_f, o_b, z, at, ga, gb, p["dn_norm"], p["w_proj_a"], p["w_proj_b"],
                      p["w_out"], tm=tm)
    return _ffn_call(x, mod, *p["ffn2"], p["final_norm"], sub=2, final=True, tm=tm)


def kernel(x_prompt, x_sample, c_prompt, c_sample, w_ada, b_ada, ffn1_norm, ffn1_w_in, ffn1_w_out, mix_norm, w_in, conv_w, a_log, dt_bias, dn_norm, attn_sink, w_proj_a, w_proj_b, w_out, ffn2_norm, ffn2_w_in, ffn2_w_out, final_norm):
    depth = w_ada.shape[0]
    layers = [
        _prepare_weights(w_ada[l], b_ada[l], ffn1_norm[l], ffn1_w_in[l], ffn1_w_out[l], mix_norm[l],
                         w_in[l], conv_w[l], a_log[l], dt_bias[l], dn_norm[l], attn_sink[l],
                         w_proj_a[l], w_proj_b[l], w_out[l], ffn2_norm[l], ffn2_w_in[l],
                         ffn2_w_out[l], final_norm)
        for l in range(depth)
    ]
    assert depth == 1, "final RMSNorm is fused into the last layer's second FFN"
    y_prompt = _trunk(x_prompt, c_prompt, layers[0])
    y_sample = _trunk(x_sample, c_sample, layers[0])
    return (y_prompt, y_sample)
```

```python
import functools
import types

import jax
import jax.numpy as jnp
from jax import lax
from jax.experimental import pallas as pl
from jax.experimental.pallas import tpu as pltpu

D_MODEL = 1024
DN_HEADS = 8
DN_DK = 128
DN_DV = 128
DN_CONV = 5
DN_CHUNK = 64
DN_SUB = 2
AT_HEADS = 8
AT_KV_HEADS = 2
AT_HEAD_DIM = 128
AT_WINDOW = 128
AT_QB = 2
ROPE_THETA = 500000.0
ROPE_DIM = AT_HEAD_DIM // 4
D_FF = 2816
EPS = 1e-6
N_MOD = 9

DN_QK_W = DN_HEADS * DN_DK
DN_V_W = DN_HEADS * DN_DV
DN_QKV_W = 2 * DN_QK_W + DN_V_W
AT_Q_W = AT_HEADS * AT_HEAD_DIM
AT_KV_W = AT_KV_HEADS * AT_HEAD_DIM
AT_GROUP = AT_HEADS // AT_KV_HEADS

LANES = 128
F32_SUBLANES = 8
VMEM_LIMIT = 56 * 1024 * 1024

F32 = jnp.float32
BF16 = jnp.bfloat16


def _dot(a, b):
    return jnp.dot(a, b, preferred_element_type=F32)


def _dot_nt(a, b):
    return lax.dot_general(a, b, (((1,), (1,)), ((), ())), preferred_element_type=F32)


def _dot_tn(a, b):
    return lax.dot_general(a, b, (((0,), (0,)), ((), ())), preferred_element_type=F32)


def _rms(x, w):
    return x * lax.rsqrt(jnp.mean(x * x, axis=-1, keepdims=True) + EPS) * w


def _ada_rms(x, w, shift, scale):
    return _rms(x, w) * (1.0 + scale) + shift


def _sigmoid(x):
    return 1.0 / (1.0 + jnp.exp(-x))


def _silu(x):
    return x * _sigmoid(x)


def _resident(shape):
    nd = len(shape)
    return pl.BlockSpec(shape, lambda *_: (0,) * nd, pipeline_mode=pl.Buffered(1))


def _params(sem):
    return pltpu.CompilerParams(dimension_semantics=sem, vmem_limit_bytes=VMEM_LIMIT)


def _mod_kernel(c_ref, w_ref, b_ref, o_ref):
    c = c_ref[...]
    o_ref[...] = _dot(_silu(c).astype(BF16), w_ref[...]) + b_ref[...]


def _mod_call(c, w_ada, b_ada):
    bsz = c.shape[0]
    n = w_ada.shape[1]
    bn = n // N_MOD
    return pl.pallas_call(
        _mod_kernel,
        grid=(N_MOD,),
        in_specs=[
            pl.BlockSpec((bsz, D_MODEL), lambda j: (0, 0)),
            pl.BlockSpec((D_MODEL, bn), lambda j: (0, j)),
            pl.BlockSpec((1, bn), lambda j: (0, j)),
        ],
        out_specs=pl.BlockSpec((bsz, bn), lambda j: (0, j)),
        out_shape=jax.ShapeDtypeStruct((bsz, n), F32),
        name="mod",
        compiler_params=_params(("arbitrary",)),
    )(c, w_ada, b_ada)


def _ffn_kernel(x_ref, mod_ref, nw_ref, wg_ref, wu_ref, wo_ref, fw_ref, o_ref, *, sub, final):
    x = x_ref[0]
    shift = mod_ref[0, 3 * sub:3 * sub + 1, :]
    scale = mod_ref[0, 3 * sub + 1:3 * sub + 2, :]
    gate = mod_ref[0, 3 * sub + 2:3 * sub + 3, :]
    h = _ada_rms(x, nw_ref[...], shift, scale).astype(BF16)
    g = _dot(h, wg_ref[...])
    u = _dot(h, wu_ref[...])
    a = (_silu(g) * u).astype(BF16)
    y = x + 0.5 * gate * _dot(a, wo_ref[...])
    if final:
        y = _rms(y, fw_ref[...])
    o_ref[0] = y


def _ffn_call(x, mod, norm_w, wg, wu, wo, final_w, *, sub, final, tm):
    bsz, t, d = x.shape
    return pl.pallas_call(
        functools.partial(_ffn_kernel, sub=sub, final=final),
        grid=(bsz, t // tm),
        in_specs=[
            pl.BlockSpec((1, tm, d), lambda b, i: (b, i, 0)),
            pl.BlockSpec((1, N_MOD, d), lambda b, i: (b, 0, 0)),
            _resident((1, d)),
            _resident(wg.shape),
            _resident(wu.shape),
            _resident(wo.shape),
            _resident((1, d)),
        ],
        out_specs=pl.BlockSpec((1, tm, d), lambda b, i: (b, i, 0)),
        out_shape=jax.ShapeDtypeStruct(x.shape, F32),
        name="ffn",
        compiler_params=_params(("parallel", "parallel")),
    )(x, mod, norm_w, wg, wu, wo, final_w)


def _rope(x, cos, sin_lo, sin_hi):
    half = ROPE_DIM // 2
    up = pltpu.roll(x, LANES - half, 1)
    down = pltpu.roll(x, half, 1)
    return x * cos + up * sin_lo + down * sin_hi


def _mix_in_kernel(x_ref, xp_ref, xn_ref, mod_ref, nw_ref, cos_ref, slo_ref, shi_ref,
                   w_qkv_ref, w_z_ref, w_ba_ref, w_q_ref, w_k_ref, w_v_ref, w_ga_ref, w_gb_ref,
                   cw_ref, alog_ref, dtb_ref,
                   qkv_ref, z_ref, bg_ref, q_ref, k_ref, v_ref, ga_ref, gb_ref, pad_ref, *, tm):
    i = pl.program_id(1)
    last = pl.num_programs(1) - 1
    halo = F32_SUBLANES
    shift, scale = mod_ref[0, 3:4, :], mod_ref[0, 4:5, :]
    x_ext = jnp.concatenate([xp_ref[0], x_ref[0], xn_ref[0]], axis=0)
    hf = _ada_rms(x_ext, nw_ref[...], shift, scale)
    h_ext = hf.astype(BF16)
    h = hf[halo:halo + tm].astype(BF16)
    cos, slo, shi = cos_ref[...], slo_ref[...], shi_ref[...]
    step = 2 * LANES
    cols = lambda c0: slice(c0, c0 + step)

    def plain(out_ref, w_ref, c0):
        out_ref[0, :, cols(c0)] = _dot(h, w_ref[:, cols(c0)]).astype(BF16)

    def roped(out_ref, w_ref, c0):
        r = _dot(h, w_ref[:, cols(c0)])
        for l0 in range(0, step, LANES):
            out_ref[0, :, c0 + l0:c0 + l0 + LANES] = _rope(r[:, l0:l0 + LANES], cos, slo, shi).astype(BF16)

    others = [functools.partial(f, o, w, c0)
              for f, o, w in ((plain, z_ref, w_z_ref), (plain, ga_ref, w_ga_ref), (plain, gb_ref, w_gb_ref),
                              (roped, q_ref, w_q_ref), (roped, k_ref, w_k_ref), (plain, v_ref, w_v_ref))
              for c0 in range(0, w.shape[1], step)]
    nchunk = DN_QKV_W // step
    for ci in range(nchunk):
        c0 = ci * step
        pad_ref[:, cols(c0)] = _dot(h_ext, w_qkv_ref[:, cols(c0)])
        pad_ref[0:halo, cols(c0)] = jnp.where(i == 0, 0.0, pad_ref[0:halo, cols(c0)])
        pad_ref[halo + tm:, cols(c0)] = jnp.where(i == last, 0.0, pad_ref[halo + tm:, cols(c0)])
        for hd in range(c0 // LANES, (c0 + step) // LANES):
            sl = slice(hd * LANES, (hd + 1) * LANES)
            acc = None
            for j in range(DN_CONV):
                off = halo - DN_CONV // 2 + j
                term = pad_ref[off:off + tm, sl] * cw_ref[j:j + 1, sl]
                acc = term if acc is None else acc + term
            y = _silu(acc)
            if hd < 2 * DN_HEADS:
                y = y * lax.rsqrt(jnp.sum(y * y, axis=-1, keepdims=True) + EPS)
                if hd < DN_HEADS:
                    y = y * (DN_DK ** -0.5)
            qkv_ref[0, :, sl] = y.astype(BF16)
        lo, hi = ci * len(others) // nchunk, (ci + 1) * len(others) // nchunk
        for work in others[lo:hi]:
            work()

    ba = _dot(h, w_ba_ref[...])
    lane = lax.broadcasted_iota(jnp.int32, ba.shape, 1)
    beta = _sigmoid(ba)
    zz = ba + dtb_ref[...]
    softplus = jnp.maximum(zz, 0.0) + jnp.log(1.0 + jnp.exp(-jnp.abs(zz)))
    g = -jnp.exp(alog_ref[...]) * softplus
    nbg = 2 * DN_HEADS
    bg_ref[0] = jnp.where(lane < nbg, beta, jnp.where(lane < 2 * nbg, g, 0.0))


def _mix_in_call(x, mod, norm_w, tabs, ws, conv_w, alog_row, dtb_row, *, tm):
    bsz, t, d = x.shape
    halo = F32_SUBLANES
    r = tm // halo
    nh = t // halo
    tok = lambda w, dt: (pl.BlockSpec((1, tm, w), lambda b, i: (b, i, 0)),
                         jax.ShapeDtypeStruct((bsz, t, w), dt))
    outs = [tok(DN_QKV_W, BF16), tok(DN_V_W, BF16), tok(LANES, F32), tok(AT_Q_W, BF16),
            tok(AT_KV_W, BF16), tok(AT_KV_W, BF16), tok(D_MODEL, BF16), tok(D_MODEL, BF16)]
    tab_spec = pl.BlockSpec((tm, LANES), lambda b, i: (i, 0))
    return pl.pallas_call(
        functools.partial(_mix_in_kernel, tm=tm),
        grid=(bsz, t // tm),
        in_specs=[
            pl.BlockSpec((1, tm, d), lambda b, i: (b, i, 0)),
            pl.BlockSpec((1, halo, d), lambda b, i: (b, jnp.maximum(i * r - 1, 0), 0)),
            pl.BlockSpec((1, halo, d), lambda b, i: (b, jnp.minimum((i + 1) * r, nh - 1), 0)),
            pl.BlockSpec((1, N_MOD, d), lambda b, i: (b, 0, 0)),
            _resident((1, d)),
            tab_spec, tab_spec, tab_spec,
        ] + [_resident(w.shape) for w in ws] + [
            _resident(conv_w.shape), _resident((1, LANES)), _resident((1, LANES)),
        ],
        out_specs=[o[0] for o in outs],
        out_shape=[o[1] for o in outs],
        scratch_shapes=[pltpu.VMEM((tm + 2 * halo, DN_QKV_W), F32)],
        name="mix_in",
        compiler_params=_params(("parallel", "parallel")),
    )(x, x, x, mod, norm_w, *tabs, *ws, conv_w, alog_row, dtb_row)


def _split2(a):
    hi = a.astype(BF16)
    lo = (a - hi.astype(F32)).astype(BF16)
    return hi, lo


def _dn_setup(qkv_ref, bg_ref, r0, *, d, reverse):
    c = DN_CHUNK
    bg = bg_ref[0, r0:r0 + c, :]
    row = lax.broadcasted_iota(jnp.int32, (c, 2 * c), 0)
    lane2 = lax.broadcasted_iota(jnp.int32, (c, 2 * c), 1)
    p_part = lane2 < c
    col = jnp.where(p_part, lane2, lane2 - c)
    if reverse:
        causal, strict = row <= col, row < col
        last = 0
    else:
        causal, strict = row >= col, row > col
        last = c - 1
    eye_t = jnp.where(p_part, 0.0, jnp.where(row == col, 1.0, 0.0)).astype(F32)
    lane = lax.broadcasted_iota(jnp.int32, bg.shape, 1)
    g = jnp.where(lane >= 2 * DN_HEADS, bg, 0.0)
    g1 = g.astype(BF16)
    r1 = g - g1.astype(F32)
    g2 = r1.astype(BF16)
    g3 = (r1 - g2.astype(F32)).astype(BF16)
    tri = jnp.where(causal[:, :c], 1.0, 0.0).astype(BF16)
    gc3 = _dot(tri, jnp.concatenate([g1, g2, g3], axis=1))
    gc = gc3[:, :LANES] + gc3[:, LANES:2 * LANES] + gc3[:, 2 * LANES:]
    gct = gc.T
    gct2 = jnp.concatenate([gct, gct], axis=1)
    e_gc = jnp.exp(gc)
    g_last = gc[last:last + 1, :]
    e_rest = jnp.exp(g_last - gc)
    e_last = jnp.exp(g_last)
    g_lane = lambda hd: 2 * DN_HEADS + d * DN_HEADS + hd
    return types.SimpleNamespace(
        causal=causal, strict=strict, p_part=p_part, eye_t=eye_t, gc=gc, gct2=gct2, e_last=e_last,
        col_of=lambda a, hd: a[:, g_lane(hd):g_lane(hd) + 1],
        row_of=lambda a, hd: a[g_lane(hd):g_lane(hd) + 1, :],
        e_gc=e_gc, e_rest=e_rest,
        beta=lambda hd: bg[:, d * DN_HEADS + hd:d * DN_HEADS + hd + 1],
        load=lambda base, hd: qkv_ref[0, r0:r0 + c, base + hd * DN_DK:base + (hd + 1) * DN_DK],
    )


def _dn_scan_kernel(qkv_f_ref, bg_f_ref, qkv_b_ref, bg_b_ref, of_ref, ob_ref,
                    s_ref, u0_ref, wq_ref, qk_ref, kd_ref, el_ref):
    c = DN_CHUNK

    @pl.when(pl.program_id(1) == 0)
    def _():
        for ref in (s_ref, u0_ref, wq_ref, qk_ref, kd_ref, el_ref):
            ref[...] = jnp.zeros_like(ref)

    out_refs = (of_ref, ob_ref)
    row0 = lambda sub, d: (DN_SUB - 1 - sub if d else sub) * c
    slot = lambda sub, d, hd: (sub * 2 + d) * DN_HEADS + hd
    pairs = [(d, hd) for hd in range(DN_HEADS) for d in (0, 1)]
    chains = [(sub, d, hd) for sub in range(DN_SUB) for d, hd in pairs]

    for sub in range(DN_SUB):
        carried = {}
        for d, hd in pairs:
            s = s_ref[d * DN_HEADS + hd]
            carried[d, hd] = (s, _dot(wq_ref[slot(sub, d, hd)], s.astype(BF16)))
        for d, hd in pairs:
            j = slot(sub, d, hd)
            s, ws = carried[d, hd]
            ub = (u0_ref[j] - ws[:c]).astype(BF16)
            r0 = row0(sub, d)
            out_refs[d][0, r0:r0 + c, hd * DN_DV:(hd + 1) * DN_DV] = ws[c:] + _dot(qk_ref[j], ub)
            s_ref[d * DN_HEADS + hd] = s * el_ref[j] + _dot_tn(kd_ref[j], ub)

    in_refs = ((qkv_f_ref, bg_f_ref), (qkv_b_ref, bg_b_ref))
    setup = {(sub, d): _dn_setup(*in_refs[d], row0(sub, d), d=d, reverse=bool(d))
             for sub in range(DN_SUB) for d in (0, 1)}
    m = {}
    for sub, d, hd in chains:
        a = setup[sub, d]
        q, k = a.load(0, hd), a.load(DN_QK_W, hd)
        gram = _dot_nt(jnp.concatenate([q, k], axis=0), jnp.concatenate([k, k], axis=0))
        diff = a.col_of(a.gc, hd) - a.row_of(a.gct2, hd)
        decay = jnp.exp(jnp.where(a.causal, diff, -jnp.inf))
        qk_ref[slot(sub, d, hd)] = (gram[:c] * decay)[:, :c].astype(BF16)
        m[sub, d, hd] = jnp.where(a.strict, -a.beta(hd) * gram[c:] * decay, 0.0)

    levels = c.bit_length() - 1
    assert 1 << levels == c
    w = {}
    for sub, d, hd in chains:
        a = setup[sub, d]
        w[sub, d, hd] = jnp.where(a.p_part, m[sub, d, hd], a.eye_t)
    for _ in range(levels):
        for key in chains:
            p_part = setup[key[0], key[1]].p_part
            wh, wl = _split2(w[key])
            p = _dot(jnp.concatenate([wh[:, :c], wl[:, :c]], axis=0),
                     jnp.concatenate([wh, wl], axis=1))
            r = p[:c, :2 * c] + p[:c, 2 * c:] + p[c:, :2 * c]
            w[key] = jnp.where(p_part, r, w[key] + r)

    for sub, d, hd in chains:
        a = setup[sub, d]
        j = slot(sub, d, hd)
        qf = a.load(0, hd).astype(F32)
        kf = a.load(DN_QK_W, hd).astype(F32)
        vf = a.load(2 * DN_QK_W, hd).astype(F32)
        eg = a.col_of(a.e_gc, hd)
        rhs = jnp.concatenate([vf * a.beta(hd), kf * (a.beta(hd) * eg)], axis=1).astype(BF16)
        sol = _dot(w[sub, d, hd].astype(BF16),
                   jnp.concatenate([jnp.zeros_like(rhs), rhs], axis=0))
        u0_ref[j] = sol[:, :DN_DV]
        wq_ref[j] = jnp.concatenate([sol[:, DN_DV:], qf * eg], axis=0).astype(BF16)
        kd_ref[j] = (kf * a.col_of(a.e_rest, hd)).astype(BF16)
        el_ref[j] = jnp.broadcast_to(a.col_of(a.e_last, hd), (1, LANES))


def _dn_scan_call(qkv, bg):
    bsz, t, w = qkv.shape
    c = DN_CHUNK
    rows = DN_SUB * c
    assert t % rows == 0, (t, rows)
    n = t // rows
    nslot = DN_SUB * 2 * DN_HEADS
    return pl.pallas_call(
        _dn_scan_kernel,
        grid=(bsz, n + 1),
        in_specs=[
            pl.BlockSpec((1, rows, w), lambda b, i: (b, jnp.minimum(i, n - 1), 0)),
            pl.BlockSpec((1, rows, LANES), lambda b, i: (b, jnp.minimum(i, n - 1), 0)),
            pl.BlockSpec((1, rows, w), lambda b, i: (b, jnp.maximum(n - 1 - i, 0), 0)),
            pl.BlockSpec((1, rows, LANES), lambda b, i: (b, jnp.maximum(n - 1 - i, 0), 0)),
        ],
        out_specs=[
            pl.BlockSpec((1, rows, DN_V_W), lambda b, i: (b, jnp.maximum(i - 1, 0), 0)),
            pl.BlockSpec((1, rows, DN_V_W), lambda b, i: (b, jnp.minimum(n - i, n - 1), 0)),
        ],
        out_shape=[jax.ShapeDtypeStruct((bsz, t, DN_V_W), F32)] * 2,
        scratch_shapes=[
            pltpu.VMEM((2 * DN_HEADS, DN_DK, DN_DV), F32),
            pltpu.VMEM((nslot, c, DN_DV), F32),
            pltpu.VMEM((nslot, 2 * c, DN_DK), BF16),
            pltpu.VMEM((nslot, c, c), BF16),
            pltpu.VMEM((nslot, c, DN_DK), BF16),
            pltpu.VMEM((nslot, 1, LANES), F32),
        ],
        name="dn_scan",
        compiler_params=_params(("parallel", "arbitrary")),
    )(qkv, bg, qkv, bg)


def _attn_kernel(sink_ref, q_ref, kp_ref, kc_ref, kn_ref, vp_ref, vc_ref, vn_ref, o_ref, *, t):
    n = pl.program_id(1)
    wb = AT_WINDOW
    kw = jnp.concatenate([kp_ref[0], kc_ref[0], kn_ref[0]], axis=0)
    vw = jnp.concatenate([vp_ref[0], vc_ref[0], vn_ref[0]], axis=0)
    qi = lax.broadcasted_iota(jnp.int32, (wb, 3 * wb), 0)
    sj = lax.broadcasted_iota(jnp.int32, (wb, 3 * wb), 1)
    inband = jnp.abs(sj - wb - qi) <= wb
    scale = AT_HEAD_DIM ** -0.5
    head_slice = lambda kv, g: slice((kv * AT_GROUP + g) * AT_HEAD_DIM, (kv * AT_GROUP + g + 1) * AT_HEAD_DIM)
    kv_slice = lambda kv: slice(kv * AT_HEAD_DIM, (kv + 1) * AT_HEAD_DIM)
    work = [(j, kv, g) for j in range(AT_QB) for kv in range(AT_KV_HEADS) for g in range(AT_GROUP)]
    bias = []
    for j in range(AT_QB):
        kpos = (n * AT_QB + j) * wb + sj - wb
        bias.append(jnp.where(inband & (kpos >= 0) & (kpos < t), 0.0, -jnp.inf).astype(F32))
    s = [_dot_nt(q_ref[0, j * wb:(j + 1) * wb, head_slice(kv, g)], kw[j * wb:(j + 3) * wb, kv_slice(kv)])
         * scale + bias[j] for j, kv, g in work]
    p, denom = [], []
    for i, (j, kv, g) in enumerate(work):
        sink = sink_ref[kv * AT_GROUP + g]
        m = jnp.maximum(jnp.max(s[i], axis=-1, keepdims=True), sink)
        e = jnp.exp(s[i] - m)
        denom.append(jnp.sum(e, axis=-1, keepdims=True) + jnp.exp(sink - m))
        p.append(e.astype(BF16))
    for i, (j, kv, g) in enumerate(work):
        o = _dot(p[i], vw[j * wb:(j + 3) * wb, kv_slice(kv)]) / denom[i]
        o_ref[0, j * wb:(j + 1) * wb, head_slice(kv, g)] = o.astype(BF16)


def _attn_call(q, k, v, sink):
    bsz, t, _ = q.shape
    wb = AT_WINDOW
    nb = t // wb
    rows = AT_QB * wb
    assert t % rows == 0, (t, rows)
    cur = lambda b, i: (b, i, 0)
    prev = lambda b, i: (b, jnp.maximum(i * AT_QB - 1, 0), 0)
    nxt = lambda b, i: (b, jnp.minimum((i + 1) * AT_QB, nb - 1), 0)
    halo_spec = lambda f: pl.BlockSpec((1, wb, AT_KV_W), f)
    cur_spec = pl.BlockSpec((1, rows, AT_KV_W), cur)
    return pl.pallas_call(
        functools.partial(_attn_kernel, t=t),
        grid=(bsz, t // rows),
        in_specs=[
            pl.BlockSpec(memory_space=pltpu.SMEM),
            pl.BlockSpec((1, rows, AT_Q_W), cur),
            halo_spec(prev), cur_spec, halo_spec(nxt),
            halo_spec(prev), cur_spec, halo_spec(nxt),
        ],
        out_specs=pl.BlockSpec((1, rows, AT_Q_W), cur),
        out_shape=jax.ShapeDtypeStruct((bsz, t, AT_Q_W), BF16),
        name="attn",
        compiler_params=_params(("parallel", "parallel")),
    )(sink, q, k, k, k, v, v, v)


def _mix_out_kernel(x_ref, mod_ref, of_ref, ob_ref, z_ref, at_ref, ga_ref, gb_ref, dnw_ref,
                    wa_ref, wb_ref, wo_ref, o_ref, dn_ref):
    x = x_ref[0]
    o = of_ref[0] + ob_ref[0]
    dnw = dnw_ref[...]
    for hd in range(DN_HEADS):
        sl = slice(hd * DN_DV, (hd + 1) * DN_DV)
        y = _rms(o[:, sl], dnw) * _silu(z_ref[0, :, sl].astype(F32))
        dn_ref[:, sl] = y.astype(BF16)
    ya = _dot(dn_ref[...], wa_ref[...])
    yb = _dot(at_ref[0], wb_ref[...])
    merged = _sigmoid(ga_ref[0].astype(F32)) * ya + _sigmoid(gb_ref[0].astype(F32)) * yb
    gate = mod_ref[0, 5:6, :]
    o_ref[0] = x + gate * _dot(merged.astype(BF16), wo_ref[...])


def _mix_out_call(x, mod, o_f, o_b, z, at, ga, gb, dn_norm, wa, wb, wo, *, tm):
    bsz, t, d = x.shape
    tok = lambda w: pl.BlockSpec((1, tm, w), lambda b, i: (b, i, 0))
    return pl.pallas_call(
        _mix_out_kernel,
        grid=(bsz, t // tm),
        in_specs=[
            tok(d),
            pl.BlockSpec((1, N_MOD, d), lambda b, i: (b, 0, 0)),
            tok(DN_V_W), tok(DN_V_W), tok(DN_V_W), tok(AT_Q_W), tok(d), tok(d),
            _resident((1, DN_DV)),
            _resident(wa.shape), _resident(wb.shape), _resident(wo.shape),
        ],
        out_specs=tok(d),
        out_shape=jax.ShapeDtypeStruct(x.shape, F32),
        scratch_shapes=[pltpu.VMEM((tm, DN_V_W), BF16)],
        name="mix_out",
        compiler_params=_params(("parallel", "parallel")),
    )(x, mod, o_f, o_b, z, at, ga, gb, dn_norm, wa, wb, wo)


def _rope_tables(t):
    half = ROPE_DIM // 2
    inv = jnp.power(jnp.float32(ROPE_THETA), -jnp.arange(half, dtype=F32) / half)
    ang = jnp.arange(t, dtype=F32)[:, None] * inv[None, :]
    cos, sin = jnp.cos(ang), jnp.sin(ang)
    rest = LANES - ROPE_DIM
    cos_t = jnp.concatenate([cos, cos, jnp.ones((t, rest), F32)], axis=1)
    zeros = jnp.zeros((t, half), F32)
    sin_lo = jnp.concatenate([-sin, zeros, jnp.zeros((t, rest), F32)], axis=1)
    sin_hi = jnp.concatenate([zeros, sin, jnp.zeros((t, rest), F32)], axis=1)
    return cos_t, sin_lo, sin_hi


def _prepare_weights(w_ada, b_ada, ffn1_norm, ffn1_w_in, ffn1_w_out, mix_norm, w_in, conv_w, a_log,
                     dt_bias, dn_norm, attn_sink, w_proj_a, w_proj_b, w_out, ffn2_norm, ffn2_w_in,
                     ffn2_w_out, final_norm):
    row = lambda a: a.reshape(1, -1).astype(F32)
    offs = [0]
    for wdt in (DN_QKV_W, DN_V_W, 2 * DN_HEADS, 2 * DN_HEADS, AT_Q_W, AT_KV_W, AT_KV_W, D_MODEL, D_MODEL):
        offs.append(offs[-1] + wdt)
    seg = lambda i, j: w_in[:, offs[i]:offs[j]]
    w_ba = jnp.pad(seg(2, 4), ((0, 0), (0, LANES - 4 * DN_HEADS)))
    mix_ws = [seg(0, 1), seg(1, 2), w_ba, seg(4, 5), seg(5, 6), seg(6, 7), seg(7, 8), seg(8, 9)]
    lane_pad = lambda a: jnp.pad(a.reshape(1, -1).astype(F32),
                                 ((0, 0), (2 * DN_HEADS, LANES - 4 * DN_HEADS)))
    return dict(
        w_ada=w_ada.astype(BF16), b_ada=row(b_ada),
        ffn1=(row(ffn1_norm), ffn1_w_in[:, :D_FF].astype(BF16), ffn1_w_in[:, D_FF:].astype(BF16),
              ffn1_w_out.astype(BF16)),
        ffn2=(row(ffn2_norm), ffn2_w_in[:, :D_FF].astype(BF16), ffn2_w_in[:, D_FF:].astype(BF16),
              ffn2_w_out.astype(BF16)),
        mix_norm=row(mix_norm), mix_ws=[w.astype(BF16) for w in mix_ws],
        conv_w=conv_w.astype(F32), a_log=lane_pad(a_log), dt_bias=lane_pad(dt_bias),
        dn_norm=row(dn_norm), sink=attn_sink.astype(F32),
        w_proj_a=w_proj_a.astype(BF16), w_proj_b=w_proj_b.astype(BF16), w_out=w_out.astype(BF16),
        final_norm=row(final_norm),
    )


def _tile(t, want):
    tile = min(t, want)
    assert t % tile == 0, (t, tile)
    return tile


def _trunk(x, c, p):
    bsz, t, d = x.shape
    tm = _tile(t, 512)
    mod = _mod_call(c, p["w_ada"], p["b_ada"]).reshape(bsz, N_MOD, d)
    x = _ffn_call(x, mod, *p["ffn1"], p["final_norm"], sub=0, final=False, tm=tm)
    qkv_n, z, bg, q, k, v, ga, gb = _mix_in_call(x, mod, p["mix_norm"], _rope_tables(t), p["mix_ws"],
                                                 p["conv_w"], p["a_log"], p["dt_bias"], tm=tm)
    o_f, o_b = _dn_scan_call(qkv_n, bg)
    at = _attn_call(q, k, v, p["sink"])
    x = _mix_out_call(x, mod, o_f, o_b, z, at, ga, gb, p["dn_norm"], p["w_proj_a"], p["w_proj_b"],
                      p["w_out"], tm=tm)
    return _ffn_call(x, mod, *p["ffn2"], p["final_norm"], sub=2, final=True, tm=tm)


def kernel(x_prompt, x_sample, c_prompt, c_sample, w_ada, b_ada, ffn1_norm, ffn1_w_in, ffn1_w_out, mix_norm, w_in, conv_w, a_log, dt_bias, dn_norm, attn_sink, w_proj_a, w_proj_b, w_out, ffn2_norm, ffn2_w_in, ffn2_w_out, final_norm):
    depth = w_ada.shape[0]
    layers = [
        _prepare_weights(w_ada[l], b_ada[l], ffn1_norm[l], ffn1_w_in[l], ffn1_w_out[l], mix_norm[l],
                         w_in[l], conv_w[l], a_log[l], dt_bias[l], dn_norm[l], attn_sink[l],
                         w_proj_a[l], w_proj_b[l], w_out[l], ffn2_norm[l], ffn2_w_in[l],
                         ffn2_w_out[l], final_norm)
        for l in range(depth)
    ]
    assert depth == 1, "final RMSNorm is fused into the last layer's second FFN"
    y_prompt = _trunk(x_prompt, c_prompt, layers[0])
    y_sample = _trunk(x_sample, c_sample, layers[0])
    return (y_prompt, y_sample)
```
